```python
import jax, jax.numpy as jnp
from jax import lax
import numpy as np

D_MODEL = 1024
BATCH = 32
SEQ = 2048
DEPTH = 2

D_A = 1024
N_HEADS = 4
HEAD_DIM = D_A // N_HEADS
QK_CONV = 4
CHUNK = 64
D_B = 1024
DW_K = 31
EPS = 1e-6

SPLIT_SIZES = (D_A, D_A, D_A, D_A, N_HEADS, N_HEADS, D_A,
               2 * D_B, D_B,
               D_MODEL, D_MODEL)
N_PROJ = int(sum(SPLIT_SIZES))
SPLIT_POINTS = tuple(int(s) for s in np.cumsum(SPLIT_SIZES)[:-1])

kernel_name = "hybrid_mlstm_conformer_conv_gated_merge"


def _rmsnorm(x, g):
    xf = x.astype(jnp.float32)
    y = xf * lax.rsqrt(jnp.mean(xf * xf, axis=-1, keepdims=True) + EPS)
    return (y * g.astype(jnp.float32)).astype(x.dtype)


def _layernorm(x, g, b):
    xf = x.astype(jnp.float32)
    mu = jnp.mean(xf, axis=-1, keepdims=True)
    var = jnp.mean(jnp.square(xf - mu), axis=-1, keepdims=True)
    y = (xf - mu) * lax.rsqrt(var + EPS)
    return (y * g.astype(jnp.float32) + b.astype(jnp.float32)).astype(x.dtype)


def _causal_dwconv(x, w, b):
    k, c = w.shape
    y = lax.conv_general_dilated(
        x, w[:, None, :].astype(x.dtype), window_strides=(1,), padding=[(k - 1, 0)],
        dimension_numbers=("NWC", "WIO", "NWC"), feature_group_count=c)
    return y + b.astype(x.dtype)


def _mlstm(q, k, v, i_pre, f_pre):
    b_, s_, h_, dh = q.shape
    nc = s_ // CHUNK

    def to_chunks(t):
        t = t.astype(jnp.float32).reshape(b_, nc, CHUNK, h_, *t.shape[3:])
        return jnp.moveaxis(t, (1, 3), (0, 2))

    qc = to_chunks(q) * (dh ** -0.5)
    kc, vc = to_chunks(k), to_chunks(v)
    lic = to_chunks(i_pre)
    lfc = to_chunks(jax.nn.log_sigmoid(f_pre.astype(jnp.float32)))
    mask = jnp.tril(jnp.ones((CHUNK, CHUNK), dtype=bool))

    def step(carry, inp):
        c_st, n_st, m_st = carry
        qx, kx, vx, li, lf = inp
        bcum = jnp.cumsum(lf, axis=-1)
        dmat = jnp.where(mask, bcum[..., :, None] - bcum[..., None, :] + li[..., None, :], -jnp.inf)
        m_inter = bcum + m_st[..., None]
        m_t = jnp.maximum(m_inter, jnp.max(dmat, axis=-1))
        scores = jnp.einsum("bhtd,bhsd->bhts", qx, kx) * jnp.exp(dmat - m_t[..., None])
        a_inter = jnp.exp(m_inter - m_t)
        num = jnp.einsum("bhts,bhse->bhte", scores, vx) + \
            a_inter[..., None] * jnp.einsum("bhtd,bhde->bhte", qx, c_st)
        den = jnp.sum(scores, axis=-1) + a_inter * jnp.einsum("bhtd,bhd->bht", qx, n_st)
        h = num / jnp.maximum(jnp.abs(den), jnp.exp(-m_t))[..., None]
        b_last = bcum[..., -1]
        g = b_last[..., None] - bcum + li
        m_new = jnp.maximum(b_last + m_st, jnp.max(g, axis=-1))
        wk = jnp.exp(g - m_new[..., None])
        a_c = jnp.exp(b_last + m_st - m_new)
        c_new = a_c[..., None, None] * c_st + jnp.einsum("bhs,bhsd,bhse->bhde", wk, kx, vx)
        n_new = a_c[..., None] * n_st + jnp.einsum("bhs,bhsd->bhd", wk, kx)
        return (c_new, n_new, m_new), h

    init = (jnp.zeros((b_, h_, dh, dh), jnp.float32),
            jnp.zeros((b_, h_, dh), jnp.float32),
            jnp.zeros((b_, h_), jnp.float32))
    _, hs = lax.scan(step, init, (qc, kc, vc, lic, lfc))
    return jnp.moveaxis(hs, (0, 2), (1, 3)).reshape(b_, s_, h_, dh)


def _layer(x, norm_g, w_in, b_if, conv_qk_w, conv_qk_b, mhn_g, dw_w, dw_b, ln_g, ln_b, w_pa, w_pb, w_out):
    b_, s_, _ = x.shape
    h = _rmsnorm(x, norm_g)
    p = jnp.einsum("bsd,dn->bsn", h, w_in.astype(h.dtype))
    q, k, v, o_pre, i_pre, f_pre, z_a, glu_in, z_b, ga_pre, gb_pre = jnp.split(p, SPLIT_POINTS, axis=-1)

    qk = jax.nn.silu(_causal_dwconv(jnp.concatenate([q, k], axis=-1), conv_qk_w, conv_qk_b))
    q, k = qk[..., :D_A], qk[..., D_A:]
    i_pre = i_pre + b_if[:N_HEADS].astype(p.dtype)
    f_pre = f_pre + b_if[N_HEADS:].astype(p.dtype)
    shp = (b_, s_, N_HEADS, HEAD_DIM)
    hm = _mlstm(q.reshape(shp), k.reshape(shp), v.reshape(shp), i_pre, f_pre)
    hm = hm * lax.rsqrt(jnp.mean(hm * hm, axis=-1, keepdims=True) + EPS)
    hm = hm.reshape(b_, s_, D_A) * mhn_g.astype(jnp.float32)
    out_a = (jax.nn.sigmoid(o_pre.astype(jnp.float32)) * hm).astype(x.dtype) * jax.nn.silu(z_a)
    y_a = jnp.einsum("bsc,cd->bsd", out_a, w_pa.astype(out_a.dtype))

    u = glu_in[..., :D_B] * jax.nn.sigmoid(glu_in[..., D_B:])
    u = _causal_dwconv(u, dw_w, dw_b)
    u = jax.nn.silu(_layernorm(u, ln_g, ln_b)) * jax.nn.silu(z_b)
    y_b = jnp.einsum("bsc,cd->bsd", u, w_pb.astype(u.dtype))

    merged = jax.nn.sigmoid(ga_pre) * y_a + jax.nn.sigmoid(gb_pre) * y_b
    return x + jnp.einsum("bsd,de->bse", merged, w_out.astype(merged.dtype))


def setup_inputs(seed: int = 0) -> dict:
    key = jax.random.key(seed)
    ks = jax.random.split(key, 16)
    L = DEPTH
    nrm = lambda k, shape, scale: jax.random.normal(k, shape, jnp.float32) * scale
    f_bias = jnp.tile(jnp.linspace(3.0, 6.0, N_HEADS, dtype=jnp.float32), (L, 1))
    b_if = jnp.concatenate([nrm(ks[3], (L, N_HEADS), 0.1),
                            f_bias + nrm(ks[4], (L, N_HEADS), 0.1)], axis=-1)
    return {
        "x": nrm(ks[0], (BATCH, SEQ, D_MODEL), 1.0),
        "norm_g": 1.0 + nrm(ks[1], (L, D_MODEL), 0.02),
        "w_in": nrm(ks[2], (L, D_MODEL, N_PROJ), D_MODEL ** -0.5),
        "b_if": b_if,
        "conv_qk_w": nrm(ks[5], (L, QK_CONV, 2 * D_A), QK_CONV ** -0.5),
        "conv_qk_b": nrm(ks[6], (L, 2 * D_A), 0.02),
        "mhn_g": 1.0 + nrm(ks[7], (L, D_A), 0.02),
        "dw_w": nrm(ks[8], (L, DW_K, D_B), DW_K ** -0.5),
        "dw_b": nrm(ks[9], (L, D_B), 0.02),
        "ln_g": 1.0 + nrm(ks[10], (L, D_B), 0.02),
        "ln_b": nrm(ks[11], (L, D_B), 0.02),
        "w_pa": nrm(ks[12], (L, D_A, D_MODEL), D_A ** -0.5),
        "w_pb": nrm(ks[13], (L, D_B, D_MODEL), D_B ** -0.5),
        "w_out": nrm(ks[14], (L, D_MODEL, D_MODEL), D_MODEL ** -0.5),
        "final_g": 1.0 + nrm(ks[15], (D_MODEL,), 0.02),
    }


def reference(x, norm_g, w_in, b_if, conv_qk_w, conv_qk_b, mhn_g, dw_w, dw_b, ln_g, ln_b,
              w_pa, w_pb, w_out, final_g):
    for l in range(DEPTH):
        x = _layer(x, norm_g[l], w_in[l], b_if[l], conv_qk_w[l], conv_qk_b[l], mhn_g[l],
                   dw_w[l], dw_b[l], ln_g[l], ln_b[l], w_pa[l], w_pb[l], w_out[l])
    return _rmsnorm(x, final_g)
```

```python
import functools

import jax
import jax.numpy as jnp
from jax import lax
from jax.experimental import pallas as pl
from jax.experimental.pallas import tpu as pltpu

D_MODEL = 1024
N_HEADS = 4
HEAD_DIM = 256
D_A = N_HEADS * HEAD_DIM
D_B = 1024
QK_CONV = 4
DW_K = 31
EPS = 1e-6

_GATE_LO = 4 * D_A
_GATE_HI = _GATE_LO + 2 * N_HEADS

TOK = 256
CHUNK = 256
QK_HIST = 8
DW_HIST = 32
XPOSE_ROWS = 128
ROWS = 32
NBLK = 256

F32 = jnp.float32
BF16 = jnp.bfloat16


def _sigmoid(v):
    return 1.0 / (1.0 + jnp.exp(-v))


def _silu(v):
    return v * _sigmoid(v)


def _lane_scan(v, op, fill):
    n = v.shape[-1]
    lane = lax.broadcasted_iota(jnp.int32, v.shape, v.ndim - 1)
    sh = 1
    while sh < n:
        shifted = pltpu.roll(v, sh, v.ndim - 1)
        v = op(v, jnp.where(lane >= sh, shifted, fill))
        sh *= 2
    return v


def _rms(v, gain):
    return (v * lax.rsqrt(jnp.mean(v * v, axis=-1, keepdims=True) + EPS)) * gain


def _layer_body(x_ref, ng_ref, wmain_ref, wif_ref, bi_ref, bf_ref, cqw_ref, cqb_ref, mhn_ref,
                dww_ref, dwb_ref, lng_ref, lnb_ref, wpa_ref, wpb_ref, wout_ref, fg_ref,
                o_ref,
                hb, qkbuf, qc, kc, vb, gate_a, ubuf, szb, sga, sgb, outa, ub, merged, cols,
                c_st, n_st, m_st,
                *, final):
    T = TOK

    @pl.when(pl.program_id(1) == 0)
    def _reset():
        qkbuf[0:QK_HIST, :] = jnp.zeros((QK_HIST, 2 * D_A), F32)
        ubuf[0:DW_HIST, :] = jnp.zeros((DW_HIST, D_B), F32)
        c_st[...] = jnp.zeros(c_st.shape, F32)
        n_st[...] = jnp.zeros(n_st.shape, F32)
        m_st[...] = jnp.zeros(m_st.shape, F32)

    def row_block(i):
        return pl.ds(pl.multiple_of(i * ROWS, ROWS), ROWS)

    def norm_blk(i, carry):
        rs = row_block(i)
        hb[rs, :] = _rms(x_ref[rs, :], ng_ref[...]).astype(BF16)
        return carry
    lax.fori_loop(0, T // ROWS, norm_blk, 0)

    def proj(blk):
        return jnp.dot(hb[...], wmain_ref[:, blk * NBLK:(blk + 1) * NBLK],
                       preferred_element_type=F32)

    nb = D_A // NBLK
    tile_rows = slice(QK_HIST, QK_HIST + T)
    conv_rows = slice(DW_HIST, DW_HIST + T)
    for c in range(nb):
        cs = slice(c * NBLK, (c + 1) * NBLK)
        qkbuf[tile_rows, c * NBLK:(c + 1) * NBLK] = proj(c)
        qkbuf[tile_rows, D_A + c * NBLK:D_A + (c + 1) * NBLK] = proj(nb + c)
        vb[:, cs] = proj(2 * nb + c).astype(BF16)
        gate_a[:, cs] = _sigmoid(proj(3 * nb + c)) * _silu(proj(4 * nb + c))
        ubuf[conv_rows, cs] = proj(5 * nb + c) * _sigmoid(proj(6 * nb + c))
        szb[:, cs] = _silu(proj(7 * nb + c))
        sga[:, cs] = _sigmoid(proj(8 * nb + c))
        sgb[:, cs] = _sigmoid(proj(9 * nb + c))
    g_t = lax.dot_general(wif_ref[...], hb[...], (((1,), (1,)), ((), ())),
                          preferred_element_type=F32)

    for rb in range(T // ROWS):
        for half, dst, scale in ((0, qc, HEAD_DIM ** -0.5), (1, kc, 1.0)):
            ls = slice(half * D_A, (half + 1) * D_A)
            acc = jnp.broadcast_to(cqb_ref[:, ls], (ROWS, D_A))
            for j in range(QK_CONV):
                r0 = QK_HIST - (QK_CONV - 1) + j + rb * ROWS
                acc = acc + qkbuf[r0:r0 + ROWS, ls] * cqw_ref[j:j + 1, ls]
            dst[rb * ROWS:(rb + 1) * ROWS, :] = (_silu(acc) * scale).astype(BF16)
    qkbuf[0:QK_HIST, :] = qkbuf[T:T + QK_HIST, :]

    tri = (lax.broadcasted_iota(jnp.int32, (CHUNK, CHUNK), 0)
           >= lax.broadcasted_iota(jnp.int32, (CHUNK, CHUNK), 1))
    for ci in range(T // CHUNK):
        r0 = ci * CHUNK
        li = g_t[0:8, r0:r0 + CHUNK] + bi_ref[...]
        fp = g_t[8:16, r0:r0 + CHUNK] + bf_ref[...]
        lf = jnp.minimum(fp, 0.0) - jnp.log1p(jnp.exp(-jnp.abs(fp)))
        bcum = _lane_scan(lf, jnp.add, 0.0)
        a = li - bcum
        cmax = _lane_scan(a, jnp.maximum, -jnp.inf)
        m_prev = m_st[:, 0:1]
        m_run = jnp.maximum(cmax, m_prev)
        m_tok = bcum + m_run
        m_last = m_run[:, CHUNK - 1:CHUNK]
        a_inter = jnp.exp(m_prev - m_run)
        e_negm = jnp.exp(-m_tok)
        wk = jnp.exp(a - m_last)
        a_carry = jnp.exp(m_prev - m_last)
        m_st[...] = jnp.broadcast_to(m_tok[:, CHUNK - 1:CHUNK], m_st.shape)

        stacked = jnp.concatenate(
            [m_run, a_inter, e_negm, wk, jnp.zeros((XPOSE_ROWS - 32, CHUNK), F32)], axis=0)
        cols[...] = stacked.T

        for hd in range(N_HEADS):
            hs = slice(hd * HEAD_DIM, (hd + 1) * HEAD_DIM)
            q = qc[r0:r0 + CHUNK, hs]
            k = kc[r0:r0 + CHUNK, hs]
            v = vb[r0:r0 + CHUNK, hs]
            m_col = cols[:, hd:hd + 1]
            ai_col = cols[:, 8 + hd:9 + hd]
            en_col = cols[:, 16 + hd:17 + hd]
            wk_col = cols[:, 24 + hd:25 + hd]

            s = lax.dot_general(q, k, (((1,), (1,)), ((), ())), preferred_element_type=F32)
            p = s * jnp.where(tri, jnp.exp(a[hd:hd + 1, :] - m_col), 0.0)
            inter = jnp.dot(q, c_st[hd].astype(BF16), preferred_element_type=F32)
            num = jnp.dot(p.astype(BF16), v, preferred_element_type=F32) + ai_col * inter
            qn = jnp.sum(q.astype(F32) * n_st[hd:hd + 1, :], axis=1, keepdims=True)
            den = jnp.sum(p, axis=1, keepdims=True) + ai_col * qn
            hh = num / jnp.maximum(jnp.abs(den), en_col)
            hh = _rms(hh, mhn_ref[:, hs])
            outa[r0:r0 + CHUNK, hs] = (hh * gate_a[r0:r0 + CHUNK, hs]).astype(BF16)

            kw = k.astype(F32) * wk_col
            upd = lax.dot_general(kw.astype(BF16), v, (((0,), (0,)), ((), ())),
                                  preferred_element_type=F32)
            ac = a_carry[hd:hd + 1, :]
            c_st[hd] = ac * c_st[hd] + upd
            n_st[hd:hd + 1, :] = ac * n_st[hd:hd + 1, :] + jnp.sum(kw, axis=0, keepdims=True)

    for rb in range(T // ROWS):
        acc = jnp.broadcast_to(dwb_ref[...], (ROWS, D_B))
        for j in range(DW_K):
            r0 = DW_HIST - (DW_K - 1) + j + rb * ROWS
            acc = acc + ubuf[r0:r0 + ROWS, :] * dww_ref[j:j + 1, :]
        mu = jnp.mean(acc, axis=-1, keepdims=True)
        cen = acc - mu
        var = jnp.mean(cen * cen, axis=-1, keepdims=True)
        yn = cen * lax.rsqrt(var + EPS) * lng_ref[...] + lnb_ref[...]
        ub[rb * ROWS:(rb + 1) * ROWS, :] = (
            _silu(yn) * szb[rb * ROWS:(rb + 1) * ROWS, :]).astype(BF16)
    ubuf[0:DW_HIST, :] = ubuf[T:T + DW_HIST, :]

    for c in range(D_MODEL // NBLK):
        cs = slice(c * NBLK, (c + 1) * NBLK)
        y_a = jnp.dot(outa[...], wpa_ref[:, cs], preferred_element_type=F32)
        y_b = jnp.dot(ub[...], wpb_ref[:, cs], preferred_element_type=F32)
        merged[:, cs] = (sga[:, cs] * y_a + sgb[:, cs] * y_b).astype(BF16)
    for c in range(D_MODEL // NBLK):
        cs = slice(c * NBLK, (c + 1) * NBLK)
        o_ref[:, cs] = x_ref[:, cs] + jnp.dot(merged[...], wout_ref[:, cs],
                                              preferred_element_type=F32)
    if final:
        def final_blk(i, carry):
            rs = row_block(i)
            o_ref[rs, :] = _rms(o_ref[rs, :], fg_ref[...])
            return carry
        lax.fori_loop(0, T // ROWS, final_blk, 0)


def _layer_call(x, ng, wmain, wif, bi, bf, cqw, cqb, mhn, dww, dwb, lng, lnb, wpa, wpb, wout, fg,
                *, final):
    B, S, D = x.shape
    T = TOK
    grid = (B, S // T)

    def whole(arr):
        nd = arr.ndim
        return pl.BlockSpec(arr.shape, lambda b, t: (0,) * nd, pipeline_mode=pl.Buffered(1))

    params = (ng, wmain, wif, bi, bf, cqw, cqb, mhn, dww, dwb, lng, lnb, wpa, wpb, wout, fg)
    tile = pl.BlockSpec((None, T, D), lambda b, t: (b, t, 0))
    scratch = [
        pltpu.VMEM((T, D_MODEL), BF16),
        pltpu.VMEM((QK_HIST + T, 2 * D_A), F32),
        pltpu.VMEM((T, D_A), BF16),
        pltpu.VMEM((T, D_A), BF16),
        pltpu.VMEM((T, D_A), BF16),
        pltpu.VMEM((T, D_A), F32),
        pltpu.VMEM((DW_HIST + T, D_B), F32),
        pltpu.VMEM((T, D_B), F32),
        pltpu.VMEM((T, D_MODEL), F32),
        pltpu.VMEM((T, D_MODEL), F32),
        pltpu.VMEM((T, D_A), BF16),
        pltpu.VMEM((T, D_B), BF16),
        pltpu.VMEM((T, D_MODEL), BF16),
        pltpu.VMEM((CHUNK, XPOSE_ROWS), F32),
        pltpu.VMEM((N_HEADS, HEAD_DIM, HEAD_DIM), F32),
        pltpu.VMEM((8, HEAD_DIM), F32),
        pltpu.VMEM((8, 128), F32),
    ]
    return pl.pallas_call(
        functools.partial(_layer_body, final=final),
        grid=grid,
        in_specs=[tile] + [whole(p) for p in params],
        out_specs=tile,
        out_shape=jax.ShapeDtypeStruct(x.shape, x.dtype),
        scratch_shapes=scratch,
        compiler_params=pltpu.CompilerParams(
            dimension_semantics=("arbitrary", "arbitrary"),
            vmem_limit_bytes=56 * 1024 * 1024),
        name="hybrid_layer_final" if final else "hybrid_layer",
    )(x, *params)


def kernel(x, norm_g, w_in, b_if, conv_qk_w, conv_qk_b, mhn_g, dw_w, dw_b, ln_g, ln_b,
           w_pa, w_pb, w_out, final_g):
    depth = norm_g.shape[0]
    row = lambda v: v.reshape(1, -1).astype(F32)
    for l in range(depth):
        w = w_in[l]
        wmain = jnp.concatenate([w[:, :_GATE_LO], w[:, _GATE_HI:]], axis=1).astype(BF16)
        wg = w[:, _GATE_LO:_GATE_HI].T
        z4 = jnp.zeros((4, D_MODEL), w.dtype)
        wif = jnp.concatenate([wg[0:4], z4, wg[4:8], z4], axis=0).astype(BF16)
        z41 = jnp.zeros((4, 1), F32)
        bi = jnp.concatenate([b_if[l, 0:4].reshape(4, 1), z41], axis=0)
        bf = jnp.concatenate([b_if[l, 4:8].reshape(4, 1), z41], axis=0)
        dww = jnp.concatenate([dw_w[l], jnp.zeros((1, D_B), dw_w.dtype)], axis=0)
        x = _layer_call(
            x, row(norm_g[l]), wmain, wif, bi, bf, conv_qk_w[l], row(conv_qk_b[l]), row(mhn_g[l]),
            dww, row(dw_b[l]), row(ln_g[l]), row(ln_b[l]),
            w_pa[l].astype(BF16), w_pb[l].astype(BF16), w_out[l].astype(BF16), row(final_g),
            final=(l == depth - 1))
    return x
```

```python
import functools

import jax
import jax.numpy as jnp
from jax import lax
from jax.experimental import pallas as pl
from jax.experimental.pallas import tpu as pltpu

D_MODEL = 1024
N_HEADS = 4
HEAD_DIM = 256
D_A = N_HEADS * HEAD_DIM
D_B = 1024
QK_CONV = 4
DW_K = 31
EPS = 1e-6

_GATE_LO = 4 * D_A
_GATE_HI = _GATE_LO + 2 * N_HEADS

TOK = 256
CHUNK = 256
QK_HIST = 8
DW_HIST = 32
XPOSE_ROWS = 128
ROWS = 32
NBLK = 256
LANES = 128
CONV_ROWS = 64

F32 = jnp.float32
BF16 = jnp.bfloat16


def _sigmoid(v):
    return 1.0 / (1.0 + jnp.exp(-v))


def _silu(v):
    return v * _sigmoid(v)


def _lane_scan(v, op, fill):
    n = v.shape[-1]
    lane = lax.broadcasted_iota(jnp.int32, v.shape, v.ndim - 1)
    sh = 1
    while sh < n:
        shifted = pltpu.roll(v, sh, v.ndim - 1)
        v = op(v, jnp.where(lane >= sh, shifted, fill))
        sh *= 2
    return v


def _rms(v, gain):
    return (v * lax.rsqrt(jnp.mean(v * v, axis=-1, keepdims=True) + EPS)) * gain


def _layer_body(x_ref, ng_ref, wmain_ref, wif_ref, bi_ref, bf_ref, cqw_ref, cqb_ref, mhn_ref,
                dww_ref, dwb_ref, lng_ref, lnb_ref, wpa_ref, wpb_ref, wout_ref, fg_ref,
                o_ref,
                hb, qkbuf, qc, kc, vb, gate_a, ubuf, cvo, szb, sga, sgb, outa, ub, merged, cols,
                c_st, n_st, m_st,
                *, final):
    T = TOK

    @pl.when(pl.program_id(1) == 0)
    def _reset():
        qkbuf[:, 0:QK_HIST, :] = jnp.zeros((qkbuf.shape[0], QK_HIST, LANES), F32)
        ubuf[:, 0:DW_HIST, :] = jnp.zeros((ubuf.shape[0], DW_HIST, LANES), F32)
        c_st[...] = jnp.zeros(c_st.shape, F32)
        n_st[...] = jnp.zeros(n_st.shape, F32)
        m_st[...] = jnp.zeros(m_st.shape, F32)

    def row_block(i):
        return pl.ds(pl.multiple_of(i * ROWS, ROWS), ROWS)

    def norm_blk(i, carry):
        rs = row_block(i)
        hb[rs, :] = _rms(x_ref[rs, :], ng_ref[...]).astype(BF16)
        return carry
    lax.fori_loop(0, T // ROWS, norm_blk, 0)

    g_t = lax.dot_general(wif_ref[...], hb[...], (((1,), (1,)), ((), ())),
                          preferred_element_type=F32)
    gate_rows = []
    m_prev = m_st[:, 0:1]
    for ci in range(T // CHUNK):
        r0 = ci * CHUNK
        li = g_t[0:8, r0:r0 + CHUNK] + bi_ref[...]
        fp = g_t[8:16, r0:r0 + CHUNK] + bf_ref[...]
        lf = jnp.minimum(fp, 0.0) - jnp.log1p(jnp.exp(-jnp.abs(fp)))
        bcum = _lane_scan(lf, jnp.add, 0.0)
        a = li - bcum
        m_run = jnp.maximum(_lane_scan(a, jnp.maximum, -jnp.inf), m_prev)
        m_tok = bcum + m_run
        m_last = m_run[:, CHUNK - 1:CHUNK]
        stacked = jnp.concatenate(
            [m_run, jnp.exp(m_prev - m_run), jnp.exp(-m_tok), jnp.exp(a - m_last),
             jnp.zeros((XPOSE_ROWS - 32, CHUNK), F32)], axis=0)
        cols[ci] = stacked.T
        gate_rows.append((a, jnp.exp(m_prev - m_last)))
        m_prev = m_tok[:, CHUNK - 1:CHUNK]
    m_st[...] = jnp.broadcast_to(m_prev, m_st.shape)

    def proj(blk):
        return jnp.dot(hb[...], wmain_ref[:, blk * NBLK:(blk + 1) * NBLK],
                       preferred_element_type=F32)

    nb = D_A // NBLK
    spb = NBLK // LANES
    tile_rows = slice(QK_HIST, QK_HIST + T)
    conv_rows = slice(DW_HIST, DW_HIST + T)

    def to_slabs(buf, first, rows, val):
        for i in range(spb):
            buf[first + i, rows, :] = val[:, i * LANES:(i + 1) * LANES]

    for c in range(nb):
        cs = slice(c * NBLK, (c + 1) * NBLK)
        to_slabs(qkbuf, c * spb, tile_rows, proj(c))
        to_slabs(qkbuf, (nb + c) * spb, tile_rows, proj(nb + c))
        vb[:, cs] = proj(2 * nb + c).astype(BF16)
        gate_a[:, cs] = _sigmoid(proj(3 * nb + c))
        gate_a[:, cs] = gate_a[:, cs] * _silu(proj(4 * nb + c))
        to_slabs(ubuf, c * spb, conv_rows, proj(5 * nb + c))
        glu_gate = _sigmoid(proj(6 * nb + c))
        for i in range(spb):
            ubuf[c * spb + i, conv_rows, :] = (
                ubuf[c * spb + i, conv_rows, :] * glu_gate[:, i * LANES:(i + 1) * LANES])
        szb[:, cs] = _silu(proj(7 * nb + c))
        sga[:, cs] = _sigmoid(proj(8 * nb + c))
        sgb[:, cs] = _sigmoid(proj(9 * nb + c))

    n_qslab = D_A // LANES
    for s in range(2 * n_qslab):
        dst, scale = (qc, HEAD_DIM ** -0.5) if s < n_qslab else (kc, 1.0)
        ls = slice(s * LANES, (s + 1) * LANES)
        ds_ = slice((s % n_qslab) * LANES, (s % n_qslab + 1) * LANES)
        for rb in range(T // CONV_ROWS):
            acc = jnp.broadcast_to(cqb_ref[:, ls], (CONV_ROWS, LANES))
            for j in range(QK_CONV):
                r0 = QK_HIST - (QK_CONV - 1) + j + rb * CONV_ROWS
                acc = acc + qkbuf[s, pl.ds(r0, CONV_ROWS), :] * cqw_ref[j:j + 1, ls]
            dst[rb * CONV_ROWS:(rb + 1) * CONV_ROWS, ds_] = (_silu(acc) * scale).astype(BF16)
    qkbuf[:, 0:QK_HIST, :] = qkbuf[:, T:T + QK_HIST, :]

    tri = (lax.broadcasted_iota(jnp.int32, (CHUNK, CHUNK), 0)
           >= lax.broadcasted_iota(jnp.int32, (CHUNK, CHUNK), 1))
    for ci in range(T // CHUNK):
        r0 = ci * CHUNK
        a, a_carry = gate_rows[ci]
        for hd in range(N_HEADS):
            hs = slice(hd * HEAD_DIM, (hd + 1) * HEAD_DIM)
            q = qc[r0:r0 + CHUNK, hs]
            k = kc[r0:r0 + CHUNK, hs]
            v = vb[r0:r0 + CHUNK, hs]
            m_col = cols[ci, :, hd:hd + 1]
            ai_col = cols[ci, :, 8 + hd:9 + hd]
            en_col = cols[ci, :, 16 + hd:17 + hd]
            wk_col = cols[ci, :, 24 + hd:25 + hd]

            s = lax.dot_general(q, k, (((1,), (1,)), ((), ())), preferred_element_type=F32)
            p = s * jnp.where(tri, jnp.exp(a[hd:hd + 1, :] - m_col), 0.0)
            inter = jnp.dot(q, c_st[hd].astype(BF16), preferred_element_type=F32)
            num = jnp.dot(p.astype(BF16), v, preferred_element_type=F32) + ai_col * inter
            qn = jnp.sum(q.astype(F32) * n_st[hd:hd + 1, :], axis=1, keepdims=True)
            den = jnp.sum(p, axis=1, keepdims=True) + ai_col * qn
            hh = num / jnp.maximum(jnp.abs(den), en_col)
            hh = _rms(hh, mhn_ref[:, hs])
            outa[r0:r0 + CHUNK, hs] = (hh * gate_a[r0:r0 + CHUNK, hs]).astype(BF16)

            kw = k.astype(F32) * wk_col
            upd = lax.dot_general(kw.astype(BF16), v, (((0,), (0,)), ((), ())),
                                  preferred_element_type=F32)
            ac = a_carry[hd:hd + 1, :]
            c_st[hd] = ac * c_st[hd] + upd
            n_st[hd:hd + 1, :] = ac * n_st[hd:hd + 1, :] + jnp.sum(kw, axis=0, keepdims=True)

    n_uslab = D_B // LANES
    half = CONV_ROWS // 2
    for s in range(n_uslab):
        ls = slice(s * LANES, (s + 1) * LANES)
        for rb in range(T // CONV_ROWS):
            for parity in range(2):
                acc = jnp.broadcast_to(dwb_ref[:, ls], (half, LANES))
                for j in range(DW_K):
                    r0 = DW_HIST - (DW_K - 1) + j + rb * CONV_ROWS + parity
                    acc = acc + ubuf[s, pl.ds(r0, half, stride=2), :] * dww_ref[j:j + 1, ls]
                cvo[s, pl.ds(rb * CONV_ROWS + parity, half, stride=2), :] = acc
    ubuf[:, 0:DW_HIST, :] = ubuf[:, T:T + DW_HIST, :]

    for rb in range(T // ROWS):
        rs = slice(rb * ROWS, (rb + 1) * ROWS)
        ys = [cvo[s, rs, :] for s in range(n_uslab)]
        mu = jnp.sum(sum(ys), axis=-1, keepdims=True) * (1.0 / D_B)
        cens = [y - mu for y in ys]
        var = jnp.sum(sum(cn * cn for cn in cens), axis=-1, keepdims=True) * (1.0 / D_B)
        inv = lax.rsqrt(var + EPS)
        for s in range(n_uslab):
            ls = slice(s * LANES, (s + 1) * LANES)
            yn = cens[s] * inv * lng_ref[:, ls] + lnb_ref[:, ls]
            ub[rs, ls] = (_silu(yn) * szb[rs, ls]).astype(BF16)

    for c in range(D_MODEL // NBLK):
        cs = slice(c * NBLK, (c + 1) * NBLK)
        y_a = jnp.dot(outa[...], wpa_ref[:, cs], preferred_element_type=F32)
        y_b = jnp.dot(ub[...], wpb_ref[:, cs], preferred_element_type=F32)
        merged[:, cs] = (sga[:, cs] * y_a + sgb[:, cs] * y_b).astype(BF16)
    for c in range(D_MODEL // NBLK):
        cs = slice(c * NBLK, (c + 1) * NBLK)
        o_ref[:, cs] = x_ref[:, cs] + jnp.dot(merged[...], wout_ref[:, cs],
                                              preferred_element_type=F32)
    if final:
        def final_blk(i, carry):
            rs = row_block(i)
            o_ref[rs, :] = _rms(o_ref[rs, :], fg_ref[...])
            return carry
        lax.fori_loop(0, T // ROWS, final_blk, 0)


def _layer_call(x, ng, wmain, wif, bi, bf, cqw, cqb, mhn, dww, dwb, lng, lnb, wpa, wpb, wout, fg,
                *, final):
    B, S, D = x.shape
    T = TOK
    grid = (B, S // T)

    def whole(arr):
        nd = arr.ndim
        return pl.BlockSpec(arr.shape, lambda b, t: (0,) * nd, pipeline_mode=pl.Buffered(1))

    params = (ng, wmain, wif, bi, bf, cqw, cqb, mhn, dww, dwb, lng, lnb, wpa, wpb, wout, fg)
    tile = pl.BlockSpec((None, T, D), lambda b, t: (b, t, 0))
    scratch = [
        pltpu.VMEM((T, D_MODEL), BF16),
        pltpu.VMEM((2 * D_A // LANES, QK_HIST + T, LANES), F32),
        pltpu.VMEM((T, D_A), BF16),
        pltpu.VMEM((T, D_A), BF16),
        pltpu.VMEM((T, D_A), BF16),
        pltpu.VMEM((T, D_A), F32),
        pltpu.VMEM((D_B // LANES, DW_HIST + T, LANES), F32),
        pltpu.VMEM((D_B // LANES, T, LANES), F32),
        pltpu.VMEM((T, D_B), F32),
        pltpu.VMEM((T, D_MODEL), F32),
        pltpu.VMEM((T, D_MODEL), F32),
        pltpu.VMEM((T, D_A), BF16),
        pltpu.VMEM((T, D_B), BF16),
        pltpu.VMEM((T, D_MODEL), BF16),
        pltpu.VMEM((T // CHUNK, CHUNK, XPOSE_ROWS), F32),
        pltpu.VMEM((N_HEADS, HEAD_DIM, HEAD_DIM), F32),
        pltpu.VMEM((8, HEAD_DIM), F32),
        pltpu.VMEM((8, 128), F32),
    ]
    return pl.pallas_call(
        functools.partial(_layer_body, final=final),
        grid=grid,
        in_specs=[tile] + [whole(p) for p in params],
        out_specs=tile,
        out_shape=jax.ShapeDtypeStruct(x.shape, x.dtype),
        scratch_shapes=scratch,
        compiler_params=pltpu.CompilerParams(
            dimension_semantics=("arbitrary", "arbitrary"),
            vmem_limit_bytes=56 * 1024 * 1024),
        name="hybrid_layer_final" if final else "hybrid_layer",
    )(x, *params)


def kernel(x, norm_g, w_in, b_if, conv_qk_w, conv_qk_b, mhn_g, dw_w, dw_b, ln_g, ln_b,
           w_pa, w_pb, w_out, final_g):
    depth = norm_g.shape[0]
    row = lambda v: v.reshape(1, -1).astype(F32)
    for l in range(depth):
        w = w_in[l]
        wmain = jnp.concatenate([w[:, :_GATE_LO], w[:, _GATE_HI:]], axis=1).astype(BF16)
        wg = w[:, _GATE_LO:_GATE_HI].T
        z4 = jnp.zeros((4, D_MODEL), w.dtype)
        wif = jnp.concatenate([wg[0:4], z4, wg[4:8], z4], axis=0).astype(BF16)
        z41 = jnp.zeros((4, 1), F32)
        bi = jnp.concatenate([b_if[l, 0:4].reshape(4, 1), z41], axis=0)
        bf = jnp.concatenate([b_if[l, 4:8].reshape(4, 1), z41], axis=0)
        dww = jnp.concatenate([dw_w[l], jnp.zeros((1, D_B), dw_w.dtype)], axis=0)
        x = _layer_call(
            x, row(norm_g[l]), wmain, wif, bi, bf, conv_qk_w[l], row(conv_qk_b[l]), row(mhn_g[l]),
            dww, row(dw_b[l]), row(ln_g[l]), row(ln_b[l]),
            w_pa[l].astype(BF16), w_pb[l].astype(BF16), w_out[l].astype(BF16), row(final_g),
            final=(l == depth - 1))
    return x
```

```python
import functools

import jax
import jax.numpy as jnp
from jax import lax
from jax.experimental import pallas as pl
from jax.experimental.pallas import tpu as pltpu

D_MODEL = 1024
N_HEADS = 4
HEAD_DIM = 256
D_A = N_HEADS * HEAD_DIM
D_B = 1024
QK_CONV = 4
DW_K = 31
EPS = 1e-6

_GATE_LO = 4 * D_A
_GATE_HI = _GATE_LO + 2 * N_HEADS
G_Q, G_K, G_V, G_O, G_ZA, G_GLU_A, G_GLU_G, G_ZB, G_GA, G_GB = range(10)

TOK = 512
CHUNK = 256
QK_HIST = 8
DW_HIST = 32
XPOSE_ROWS = 128
ROWS = 32
NBLK = 256
LANES = 128
CONV_ROWS = 64
MROWS = 128
DW_PER_TASK = 4
VMEM_LIMIT = 62 * 1024 * 1024

F32 = jnp.float32
BF16 = jnp.bfloat16


def _sigmoid(v):
    return 1.0 / (1.0 + jnp.exp(-v))


def _silu(v):
    return v * _sigmoid(v)


def _lane_scan(v, op, fill):
    n = v.shape[-1]
    lane = lax.broadcasted_iota(jnp.int32, v.shape, v.ndim - 1)
    sh = 1
    while sh < n:
        shifted = pltpu.roll(v, sh, v.ndim - 1)
        v = op(v, jnp.where(lane >= sh, shifted, fill))
        sh *= 2
    return v


def _rms(v, gain):
    return (v * lax.rsqrt(jnp.mean(v * v, axis=-1, keepdims=True) + EPS)) * gain


def _layer_body(x_ref, ng_ref, wmain_ref, wif_ref, bi_ref, bf_ref, cqw_ref, cqb_ref, mhn_ref,
                dww_ref, dwb_ref, lng_ref, lnb_ref, wpa_ref, wpb_ref, wout_ref, fg_ref,
                o_ref,
                hb, qkbuf, qc, kc, vb, ubuf, cvo, hhb, outa, ub, merged, cols,
                c_st, c_bf, n_st, m_st,
                *, final):
    T = TOK
    nb = D_A // NBLK
    spb = NBLK // LANES
    n_qslab = D_A // LANES
    n_chunks = T // CHUNK
    tile_rows = slice(QK_HIST, QK_HIST + T)
    conv_rows = slice(DW_HIST, DW_HIST + T)

    @pl.when(pl.program_id(1) == 0)
    def _reset():
        qkbuf[:, 0:QK_HIST, :] = jnp.zeros((qkbuf.shape[0], QK_HIST, LANES), F32)
        ubuf[:, 0:DW_HIST, :] = jnp.zeros((ubuf.shape[0], DW_HIST, LANES), F32)
        c_st[...] = jnp.zeros(c_st.shape, F32)
        c_bf[...] = jnp.zeros(c_bf.shape, BF16)
        n_st[...] = jnp.zeros(n_st.shape, F32)
        m_st[...] = jnp.zeros(m_st.shape, F32)

    def row_block(i):
        return pl.ds(pl.multiple_of(i * ROWS, ROWS), ROWS)

    def col_block(c):
        return slice(c * NBLK, (c + 1) * NBLK)

    def lane_block(i):
        return slice(i * LANES, (i + 1) * LANES)

    def proj(group, c):
        blk = group * nb + c
        return jnp.dot(hb[...], wmain_ref[:, blk * NBLK:(blk + 1) * NBLK],
                       preferred_element_type=F32)

    def to_slabs(buf, first, rows, val):
        for i in range(spb):
            buf[first + i, rows, :] = val[:, lane_block(i)]

    def scale_slabs(buf, first, rows, val):
        for i in range(spb):
            buf[first + i, rows, :] = buf[first + i, rows, :] * val[:, lane_block(i)]

    def norm_blk(i, carry):
        rs = row_block(i)
        hb[rs, :] = _rms(x_ref[rs, :], ng_ref[...]).astype(BF16)
        return carry
    lax.fori_loop(0, T // ROWS, norm_blk, 0)

    g_t = lax.dot_general(wif_ref[...], hb[...], (((1,), (1,)), ((), ())),
                          preferred_element_type=F32)
    gate_rows = []
    m_prev = m_st[:, 0:1]
    for ci in range(n_chunks):
        r0 = ci * CHUNK
        li = g_t[0:8, r0:r0 + CHUNK] + bi_ref[...]
        fp = g_t[8:16, r0:r0 + CHUNK] + bf_ref[...]
        lf = jnp.minimum(fp, 0.0) - jnp.log1p(jnp.exp(-jnp.abs(fp)))
        bcum = _lane_scan(lf, jnp.add, 0.0)
        a = li - bcum
        m_run = jnp.maximum(_lane_scan(a, jnp.maximum, -jnp.inf), m_prev)
        m_tok = bcum + m_run
        m_last = m_run[:, CHUNK - 1:CHUNK]
        stacked = jnp.concatenate(
            [m_run, jnp.exp(m_prev - m_run), jnp.exp(-m_tok), jnp.exp(a - m_last),
             jnp.zeros((XPOSE_ROWS - 32, CHUNK), F32)], axis=0)
        cols[ci] = stacked.T
        gate_rows.append((a, jnp.exp(m_prev - m_last)))
        m_prev = m_tok[:, CHUNK - 1:CHUNK]
    m_st[...] = jnp.broadcast_to(m_prev, m_st.shape)

    def qkconv_slab(s):
        dst, scale = (qc, HEAD_DIM ** -0.5) if s < n_qslab else (kc, 1.0)
        ls = lane_block(s)
        ds_ = lane_block(s % n_qslab)
        for rb in range(T // CONV_ROWS):
            acc = jnp.broadcast_to(cqb_ref[:, ls], (CONV_ROWS, LANES))
            for j in range(QK_CONV):
                r0 = QK_HIST - (QK_CONV - 1) + j + rb * CONV_ROWS
                acc = acc + qkbuf[s, pl.ds(r0, CONV_ROWS), :] * cqw_ref[j:j + 1, ls]
            dst[rb * CONV_ROWS:(rb + 1) * CONV_ROWS, ds_] = (_silu(acc) * scale).astype(BF16)

    def dwconv_unit(s, rb, parity):
        ls = lane_block(s)
        half = CONV_ROWS // 2
        acc = jnp.broadcast_to(dwb_ref[:, ls], (half, LANES))
        for j in range(DW_K):
            r0 = DW_HIST - (DW_K - 1) + j + rb * CONV_ROWS + parity
            acc = acc + ubuf[s, pl.ds(r0, half, stride=2), :] * dww_ref[j:j + 1, ls]
        cvo[s, pl.ds(rb * CONV_ROWS + parity, half, stride=2), :] = acc

    dw_queue = [(s, rb, parity) for s in range(D_B // LANES)
                for rb in range(T // CONV_ROWS) for parity in range(2)]

    def emit_dw(n):
        for _ in range(min(n, len(dw_queue))):
            dwconv_unit(*dw_queue.pop(0))

    def ln_rows(rb):
        rs = slice(rb * ROWS, (rb + 1) * ROWS)
        n_slab = D_B // LANES
        ys = [cvo[s, rs, :] for s in range(n_slab)]
        mu = jnp.sum(sum(ys), axis=-1, keepdims=True) * (1.0 / D_B)
        cens = [y - mu for y in ys]
        var = jnp.sum(sum(cn * cn for cn in cens), axis=-1, keepdims=True) * (1.0 / D_B)
        inv = lax.rsqrt(var + EPS)
        for s in range(n_slab):
            cvo[s, rs, :] = _silu(cens[s] * inv * lng_ref[:, lane_block(s)] + lnb_ref[:, lane_block(s)])

    def mlstm(ci, hd):
        r0 = ci * CHUNK
        a, a_carry = gate_rows[ci]
        hs = col_block(hd)
        k = kc[r0:r0 + CHUNK, hs]
        v = vb[r0:r0 + CHUNK, hs]
        col_iota = lax.broadcasted_iota(jnp.int32, (MROWS, CHUNK), 1)
        for part in range(CHUNK // MROWS):
            lr = slice(part * MROWS, (part + 1) * MROWS)
            rr = slice(r0 + part * MROWS, r0 + (part + 1) * MROWS)
            q = qc[rr, hs]
            m_col = cols[ci, lr, hd:hd + 1]
            ai_col = cols[ci, lr, 8 + hd:9 + hd]
            en_col = cols[ci, lr, 16 + hd:17 + hd]
            causal = (lax.broadcasted_iota(jnp.int32, (MROWS, CHUNK), 0) + part * MROWS) >= col_iota

            s = lax.dot_general(q, k, (((1,), (1,)), ((), ())), preferred_element_type=F32)
            p = s * jnp.where(causal, jnp.exp(a[hd:hd + 1, :] - m_col), 0.0)
            inter = jnp.dot(q, c_bf[hd], preferred_element_type=F32)
            num = jnp.dot(p.astype(BF16), v, preferred_element_type=F32) + ai_col * inter
            qn = jnp.sum(q.astype(F32) * n_st[hd:hd + 1, :], axis=1, keepdims=True)
            den = jnp.sum(p, axis=1, keepdims=True) + ai_col * qn
            hh = num / jnp.maximum(jnp.abs(den), en_col)
            hhb[rr, :] = _rms(hh, mhn_ref[:, hs])

        kw = k.astype(F32) * cols[ci, :, 24 + hd:25 + hd]
        upd = lax.dot_general(kw.astype(BF16), v, (((0,), (0,)), ((), ())),
                              preferred_element_type=F32)
        ac = a_carry[hd:hd + 1, :]
        c_new = ac * c_st[hd] + upd
        c_st[hd] = c_new
        c_bf[hd] = c_new.astype(BF16)
        n_st[hd:hd + 1, :] = ac * n_st[hd:hd + 1, :] + jnp.sum(kw, axis=0, keepdims=True)

    for c in range(nb):
        to_slabs(ubuf, c * spb, conv_rows, proj(G_GLU_A, c))
        scale_slabs(ubuf, c * spb, conv_rows, _sigmoid(proj(G_GLU_G, c)))
        emit_dw(DW_PER_TASK)
    for c in range(nb):
        to_slabs(qkbuf, c * spb, tile_rows, proj(G_Q, c))
        emit_dw(DW_PER_TASK)
        to_slabs(qkbuf, (nb + c) * spb, tile_rows, proj(G_K, c))
        emit_dw(DW_PER_TASK)
    for c in range(nb):
        vb[:, col_block(c)] = proj(G_V, c).astype(BF16)
        for i in range(spb):
            qkconv_slab(c * spb + i)
            qkconv_slab(n_qslab + c * spb + i)
        emit_dw(DW_PER_TASK)
    qkbuf[:, 0:QK_HIST, :] = qkbuf[:, T:T + QK_HIST, :]

    for hd in range(N_HEADS):
        for ci in range(n_chunks):
            mlstm(ci, hd)
            emit_dw(DW_PER_TASK)
        hhb[...] = hhb[...] * _sigmoid(proj(G_O, hd))
        emit_dw(DW_PER_TASK)
        outa[:, col_block(hd)] = (hhb[...] * _silu(proj(G_ZA, hd))).astype(BF16)
        emit_dw(DW_PER_TASK)
        to_slabs(qkbuf, hd * spb, tile_rows, _sigmoid(proj(G_GA, hd)))
        emit_dw(DW_PER_TASK)
        to_slabs(qkbuf, (nb + hd) * spb, tile_rows, _sigmoid(proj(G_GB, hd)))
        emit_dw(DW_PER_TASK)
    emit_dw(len(dw_queue))
    ubuf[:, 0:DW_HIST, :] = ubuf[:, T:T + DW_HIST, :]

    n_ln = T // ROWS
    for c in range(nb):
        scale_slabs(qkbuf, c * spb, tile_rows,
                    jnp.dot(outa[...], wpa_ref[:, col_block(c)], preferred_element_type=F32))
        for rb in range(c * n_ln // nb, (c + 1) * n_ln // nb):
            ln_rows(rb)
    for c in range(nb):
        gate = _silu(proj(G_ZB, c))
        for i in range(spb):
            ub[:, lane_block(c * spb + i)] = (cvo[c * spb + i] * gate[:, lane_block(i)]).astype(BF16)
    for c in range(nb):
        y_b = jnp.dot(ub[...], wpb_ref[:, col_block(c)], preferred_element_type=F32)
        for i in range(spb):
            merged[:, lane_block(c * spb + i)] = (
                qkbuf[c * spb + i, tile_rows, :]
                + qkbuf[(nb + c) * spb + i, tile_rows, :] * y_b[:, lane_block(i)]).astype(BF16)
    for c in range(D_MODEL // NBLK):
        cs = col_block(c)
        o_ref[:, cs] = x_ref[:, cs] + jnp.dot(merged[...], wout_ref[:, cs],
                                              preferred_element_type=F32)
    if final:
        def final_blk(i, carry):
            rs = row_block(i)
            o_ref[rs, :] = _rms(o_ref[rs, :], fg_ref[...])
            return carry
        lax.fori_loop(0, T // ROWS, final_blk, 0)


def _layer_call(x, ng, wmain, wif, bi, bf, cqw, cqb, mhn, dww, dwb, lng, lnb, wpa, wpb, wout, fg,
                *, final):
    B, S, D = x.shape
    T = TOK
    grid = (B, S // T)

    def whole(arr):
        nd = arr.ndim
        return pl.BlockSpec(arr.shape, lambda b, t: (0,) * nd, pipeline_mode=pl.Buffered(1))

    params = (ng, wmain, wif, bi, bf, cqw, cqb, mhn, dww, dwb, lng, lnb, wpa, wpb, wout, fg)
    tile = pl.BlockSpec((None, T, D), lambda b, t: (b, t, 0))
    scratch = [
        pltpu.VMEM((T, D_MODEL), BF16),
        pltpu.VMEM((2 * D_A // LANES, QK_HIST + T, LANES), F32),
        pltpu.VMEM((T, D_A), BF16),
        pltpu.VMEM((T, D_A), BF16),
        pltpu.VMEM((T, D_A), BF16),
        pltpu.VMEM((D_B // LANES, DW_HIST + T, LANES), F32),
        pltpu.VMEM((D_B // LANES, T, LANES), F32),
        pltpu.VMEM((T, HEAD_DIM), F32),
        pltpu.VMEM((T, D_A), BF16),
        pltpu.VMEM((T, D_B), BF16),
        pltpu.VMEM((T, D_MODEL), BF16),
        pltpu.VMEM((T // CHUNK, CHUNK, XPOSE_ROWS), F32),
        pltpu.VMEM((N_HEADS, HEAD_DIM, HEAD_DIM), F32),
        pltpu.VMEM((N_HEADS, HEAD_DIM, HEAD_DIM), BF16),
        pltpu.VMEM((8, HEAD_DIM), F32),
        pltpu.VMEM((8, 128), F32),
    ]
    return pl.pallas_call(
        functools.partial(_layer_body, final=final),
        grid=grid,
        in_specs=[tile] + [whole(p) for p in params],
        out_specs=tile,
        out_shape=jax.ShapeDtypeStruct(x.shape, x.dtype),
        scratch_shapes=scratch,
        compiler_params=pltpu.CompilerParams(
            dimension_semantics=("arbitrary", "arbitrary"),
            vmem_limit_bytes=VMEM_LIMIT),
        name="hybrid_layer_final" if final else "hybrid_layer",
    )(x, *params)


def kernel(x, norm_g, w_in, b_if, conv_qk_w, conv_qk_b, mhn_g, dw_w, dw_b, ln_g, ln_b,
           w_pa, w_pb, w_out, final_g):
    depth = norm_g.shape[0]
    row = lambda v: v.reshape(1, -1).astype(F32)
    for l in range(depth):
        w = w_in[l]
        wmain = jnp.concatenate([w[:, :_GATE_LO], w[:, _GATE_HI:]], axis=1).astype(BF16)
        wg = w[:, _GATE_LO:_GATE_HI].T
        z4 = jnp.zeros((4, D_MODEL), w.dtype)
        wif = jnp.concatenate([wg[0:4], z4, wg[4:8], z4], axis=0).astype(BF16)
        z41 = jnp.zeros((4, 1), F32)
        bi = jnp.concatenate([b_if[l, 0:4].reshape(4, 1), z41], axis=0)
        bf = jnp.concatenate([b_if[l, 4:8].reshape(4, 1), z41], axis=0)
        dww = jnp.concatenate([dw_w[l], jnp.zeros((1, D_B), dw_w.dtype)], axis=0)
        x = _layer_call(
            x, row(norm_g[l]), wmain, wif, bi, bf, conv_qk_w[l], row(conv_qk_b[l]), row(mhn_g[l]),
            dww, row(dw_b[l]), row(ln_g[l]), row(ln_b[l]),
            w_pa[l].astype(BF16), w_pb[l].astype(BF16), w_out[l].astype(BF16), row(final_g),
            final=(l == depth - 1))
    return x
```

```python
import functools

import jax
import jax.numpy as jnp
from jax import lax
from jax.experimental import pallas as pl
from jax.experimental.pallas import tpu as pltpu

D_MODEL = 1024
N_HEADS = 4
HEAD_DIM = 256
D_A = N_HEADS * HEAD_DIM
D_B = 1024
QK_CONV = 4
DW_K = 31
EPS = 1e-6

_GATE_LO = 4 * D_A
_GATE_HI = _GATE_LO + 2 * N_HEADS
G_Q, G_K, G_V, G_O, G_ZA, G_GLU_A, G_GLU_G, G_ZB, G_GA, G_GB = range(10)

TOK = 512
CHUNK = 256
QK_HIST = 8
DW_HIST = 32
XPOSE_ROWS = 128
ROWS = 32
NBLK = 256
LANES = 128
CONV_ROWS = 64
MROWS = 128
VMEM_LIMIT = 62 * 1024 * 1024

F32 = jnp.float32
BF16 = jnp.bfloat16


def _sigmoid(v):
    return 1.0 / (1.0 + jnp.exp(-v))


def _silu(v):
    return v * _sigmoid(v)


def _lane_scan(v, op, fill):
    n = v.shape[-1]
    lane = lax.broadcasted_iota(jnp.int32, v.shape, v.ndim - 1)
    sh = 1
    while sh < n:
        shifted = pltpu.roll(v, sh, v.ndim - 1)
        v = op(v, jnp.where(lane >= sh, shifted, fill))
        sh *= 2
    return v


def _rms(v, gain):
    return (v * lax.rsqrt(jnp.mean(v * v, axis=-1, keepdims=True) + EPS)) * gain


def _layer_body(x_ref, ng_ref, wmain_ref, wif_ref, bi_ref, bf_ref, cqw_ref, cqb_ref, mhn_ref,
                dww_ref, dwb_ref, lng_ref, lnb_ref, wpa_ref, wpb_ref, wout_ref, fg_ref,
                o_ref,
                hb, qkbuf, qc, kc, vb, ubuf, cvo, hhb, outa, ub, merged, cols,
                c_st, c_bf, n_st, m_st,
                *, final):
    T = TOK
    nb = D_A // NBLK
    spb = NBLK // LANES
    n_qslab = D_A // LANES
    n_chunks = T // CHUNK
    tile_rows = slice(QK_HIST, QK_HIST + T)
    conv_rows = slice(DW_HIST, DW_HIST + T)

    @pl.when(pl.program_id(1) == 0)
    def _reset():
        qkbuf[:, 0:QK_HIST, :] = jnp.zeros((qkbuf.shape[0], QK_HIST, LANES), F32)
        ubuf[:, 0:DW_HIST, :] = jnp.zeros((ubuf.shape[0], DW_HIST, LANES), F32)
        c_st[...] = jnp.zeros(c_st.shape, F32)
        c_bf[...] = jnp.zeros(c_bf.shape, BF16)
        n_st[...] = jnp.zeros(n_st.shape, F32)
        m_st[...] = jnp.zeros(m_st.shape, F32)

    def col_block(c):
        return slice(c * NBLK, (c + 1) * NBLK)

    def lane_block(i):
        return slice(i * LANES, (i + 1) * LANES)

    def proj(group, c):
        blk = group * nb + c
        return jnp.dot(hb[...], wmain_ref[:, blk * NBLK:(blk + 1) * NBLK],
                       preferred_element_type=F32)

    def to_slabs(buf, first, rows, val):
        for i in range(spb):
            buf[first + i, rows, :] = val[:, lane_block(i)]

    def scale_slabs(buf, first, rows, val):
        for i in range(spb):
            buf[first + i, rows, :] = buf[first + i, rows, :] * val[:, lane_block(i)]

    for i in range(T // ROWS):
        rs = slice(i * ROWS, (i + 1) * ROWS)
        hb[rs, :] = _rms(x_ref[rs, :], ng_ref[...]).astype(BF16)

    g_t = lax.dot_general(wif_ref[...], hb[...], (((1,), (1,)), ((), ())),
                          preferred_element_type=F32)
    gate_rows = []
    m_prev = m_st[:, 0:1]
    for ci in range(n_chunks):
        r0 = ci * CHUNK
        li = g_t[0:8, r0:r0 + CHUNK] + bi_ref[...]
        fp = g_t[8:16, r0:r0 + CHUNK] + bf_ref[...]
        lf = jnp.minimum(fp, 0.0) - jnp.log1p(jnp.exp(-jnp.abs(fp)))
        bcum = _lane_scan(lf, jnp.add, 0.0)
        a = li - bcum
        m_run = jnp.maximum(_lane_scan(a, jnp.maximum, -jnp.inf), m_prev)
        m_tok = bcum + m_run
        m_last = m_run[:, CHUNK - 1:CHUNK]
        stacked = jnp.concatenate(
            [m_run, jnp.exp(m_prev - m_run), jnp.exp(-m_tok), jnp.exp(a - m_last),
             jnp.zeros((XPOSE_ROWS - 32, CHUNK), F32)], axis=0)
        cols[ci] = stacked.T
        gate_rows.append((a, jnp.exp(m_prev - m_last)))
        m_prev = m_tok[:, CHUNK - 1:CHUNK]
    m_st[...] = jnp.broadcast_to(m_prev, m_st.shape)

    def qkconv_slab(s):
        dst, scale = (qc, HEAD_DIM ** -0.5) if s < n_qslab else (kc, 1.0)
        ls = lane_block(s)
        ds_ = lane_block(s % n_qslab)
        for rb in range(T // CONV_ROWS):
            acc = jnp.broadcast_to(cqb_ref[:, ls], (CONV_ROWS, LANES))
            for j in range(QK_CONV):
                r0 = QK_HIST - (QK_CONV - 1) + j + rb * CONV_ROWS
                acc = acc + qkbuf[s, pl.ds(r0, CONV_ROWS), :] * cqw_ref[j:j + 1, ls]
            dst[rb * CONV_ROWS:(rb + 1) * CONV_ROWS, ds_] = (_silu(acc) * scale).astype(BF16)

    def dwconv_unit(s, rb):
        ls = lane_block(s)
        half = CONV_ROWS // 2
        accs = [jnp.broadcast_to(dwb_ref[:, ls], (half, LANES))] * 2
        for j in range(DW_K):
            w = dww_ref[j:j + 1, ls]
            for parity in range(2):
                r0 = DW_HIST - (DW_K - 1) + j + rb * CONV_ROWS + parity
                accs[parity] = accs[parity] + ubuf[s, pl.ds(r0, half, stride=2), :] * w
        for parity in range(2):
            cvo[s, pl.ds(rb * CONV_ROWS + parity, half, stride=2), :] = accs[parity]

    dw_queue = [(s, rb) for s in range(D_B // LANES) for rb in range(T // CONV_ROWS)]

    def emit_dw(n):
        for _ in range(min(n, len(dw_queue))):
            dwconv_unit(*dw_queue.pop(0))

    def ln_rows(rb):
        rs = slice(rb * ROWS, (rb + 1) * ROWS)
        n_slab = D_B // LANES
        ys = [cvo[s, rs, :] for s in range(n_slab)]
        mu = jnp.sum(sum(ys), axis=-1, keepdims=True) * (1.0 / D_B)
        cens = [y - mu for y in ys]
        var = jnp.sum(sum(cn * cn for cn in cens), axis=-1, keepdims=True) * (1.0 / D_B)
        inv = lax.rsqrt(var + EPS)
        for s in range(n_slab):
            cvo[s, rs, :] = _silu(cens[s] * inv * lng_ref[:, lane_block(s)] + lnb_ref[:, lane_block(s)])

    def mlstm(ci, hd):
        r0 = ci * CHUNK
        a, a_carry = gate_rows[ci]
        hs = col_block(hd)
        k = kc[r0:r0 + CHUNK, hs]
        v = vb[r0:r0 + CHUNK, hs]
        col_iota = lax.broadcasted_iota(jnp.int32, (MROWS, CHUNK), 1)
        for part in range(CHUNK // MROWS):
            lr = slice(part * MROWS, (part + 1) * MROWS)
            rr = slice(r0 + part * MROWS, r0 + (part + 1) * MROWS)
            q = qc[rr, hs]
            m_col = cols[ci, lr, hd:hd + 1]
            ai_col = cols[ci, lr, 8 + hd:9 + hd]
            en_col = cols[ci, lr, 16 + hd:17 + hd]
            causal = (lax.broadcasted_iota(jnp.int32, (MROWS, CHUNK), 0) + part * MROWS) >= col_iota

            s = lax.dot_general(q, k, (((1,), (1,)), ((), ())), preferred_element_type=F32)
            p = s * jnp.where(causal, jnp.exp(a[hd:hd + 1, :] - m_col), 0.0)
            inter = jnp.dot(q, c_bf[hd], preferred_element_type=F32)
            num = jnp.dot(p.astype(BF16), v, preferred_element_type=F32) + ai_col * inter
            qn = jnp.sum(q.astype(F32) * n_st[hd:hd + 1, :], axis=1, keepdims=True)
            den = jnp.sum(p, axis=1, keepdims=True) + ai_col * qn
            hh = num / jnp.maximum(jnp.abs(den), en_col)
            hhb[rr, :] = _rms(hh, mhn_ref[:, hs])

        kw = k.astype(F32) * cols[ci, :, 24 + hd:25 + hd]
        upd = lax.dot_general(kw.astype(BF16), v, (((0,), (0,)), ((), ())),
                              preferred_element_type=F32)
        ac = a_carry[hd:hd + 1, :]
        c_new = ac * c_st[hd] + upd
        c_st[hd] = c_new
        c_bf[hd] = c_new.astype(BF16)
        n_st[hd:hd + 1, :] = ac * n_st[hd:hd + 1, :] + jnp.sum(kw, axis=0, keepdims=True)

    units_per_slab = T // CONV_ROWS
    for c in range(nb):
        to_slabs(ubuf, c * spb, conv_rows, proj(G_GLU_A, c))
        scale_slabs(ubuf, c * spb, conv_rows, _sigmoid(proj(G_GLU_G, c)))
        if c > 0:
            emit_dw(units_per_slab)
    per_dot = -(-len(dw_queue) // (2 * nb))
    for c in range(nb):
        to_slabs(qkbuf, c * spb, tile_rows, proj(G_Q, c))
        emit_dw(per_dot)
        to_slabs(qkbuf, (nb + c) * spb, tile_rows, proj(G_K, c))
        emit_dw(per_dot)
    ubuf[:, 0:DW_HIST, :] = ubuf[:, T:T + DW_HIST, :]
    for c in range(nb):
        vb[:, col_block(c)] = proj(G_V, c).astype(BF16)
        for i in range(spb):
            qkconv_slab(c * spb + i)
            qkconv_slab(n_qslab + c * spb + i)
    qkbuf[:, 0:QK_HIST, :] = qkbuf[:, T:T + QK_HIST, :]

    for hd in range(N_HEADS):
        for ci in range(n_chunks):
            mlstm(ci, hd)
        hhb[...] = hhb[...] * _sigmoid(proj(G_O, hd))
        outa[:, col_block(hd)] = (hhb[...] * _silu(proj(G_ZA, hd))).astype(BF16)
        to_slabs(qkbuf, hd * spb, tile_rows, _sigmoid(proj(G_GA, hd)))
        to_slabs(qkbuf, (nb + hd) * spb, tile_rows, _sigmoid(proj(G_GB, hd)))

    n_ln = T // ROWS
    for c in range(nb):
        scale_slabs(qkbuf, c * spb, tile_rows,
                    jnp.dot(outa[...], wpa_ref[:, col_block(c)], preferred_element_type=F32))
        for rb in range(c * n_ln // nb, (c + 1) * n_ln // nb):
            ln_rows(rb)
    for c in range(nb):
        gate = _silu(proj(G_ZB, c))
        for i in range(spb):
            ub[:, lane_block(c * spb + i)] = (cvo[c * spb + i] * gate[:, lane_block(i)]).astype(BF16)
    for c in range(nb):
        y_b = jnp.dot(ub[...], wpb_ref[:, col_block(c)], preferred_element_type=F32)
        for i in range(spb):
            merged[:, lane_block(c * spb + i)] = (
                qkbuf[c * spb + i, tile_rows, :]
                + qkbuf[(nb + c) * spb + i, tile_rows, :] * y_b[:, lane_block(i)]).astype(BF16)
    for c in range(D_MODEL // NBLK):
        cs = col_block(c)
        o_ref[:, cs] = x_ref[:, cs] + jnp.dot(merged[...], wout_ref[:, cs],
                                              preferred_element_type=F32)
    if final:
        for i in range(T // ROWS):
            rs = slice(i * ROWS, (i + 1) * ROWS)
            o_ref[rs, :] = _rms(o_ref[rs, :], fg_ref[...])


def _layer_call(x, ng, wmain, wif, bi, bf, cqw, cqb, mhn, dww, dwb, lng, lnb, wpa, wpb, wout, fg,
                *, final):
    B, S, D = x.shape
    T = TOK
    grid = (B, S // T)

    def whole(arr):
        nd = arr.ndim
        return pl.BlockSpec(arr.shape, lambda b, t: (0,) * nd, pipeline_mode=pl.Buffered(1))

    params = (ng, wmain, wif, bi, bf, cqw, cqb, mhn, dww, dwb, lng, lnb, wpa, wpb, wout, fg)
    tile = pl.BlockSpec((None, T, D), lambda b, t: (b, t, 0))
    scratch = [
        pltpu.VMEM((T, D_MODEL), BF16),
        pltpu.VMEM((2 * D_A // LANES, QK_HIST + T, LANES), F32),
        pltpu.VMEM((T, D_A), BF16),
        pltpu.VMEM((T, D_A), BF16),
        pltpu.VMEM((T, D_A), BF16),
        pltpu.VMEM((D_B // LANES, DW_HIST + T, LANES), F32),
        pltpu.VMEM((D_B // LANES, T, LANES), F32),
        pltpu.VMEM((T, HEAD_DIM), F32),
        pltpu.VMEM((T, D_A), BF16),
        pltpu.VMEM((T, D_B), BF16),
        pltpu.VMEM((T, D_MODEL), BF16),
        pltpu.VMEM((T // CHUNK, CHUNK, XPOSE_ROWS), F32),
        pltpu.VMEM((N_HEADS, HEAD_DIM, HEAD_DIM), F32),
        pltpu.VMEM((N_HEADS, HEAD_DIM, HEAD_DIM), BF16),
        pltpu.VMEM((8, HEAD_DIM), F32),
        pltpu.VMEM((8, 128), F32),
    ]
    return pl.pallas_call(
        functools.partial(_layer_body, final=final),
        grid=grid,
        in_specs=[tile] + [whole(p) for p in params],
        out_specs=tile,
        out_shape=jax.ShapeDtypeStruct(x.shape, x.dtype),
        scratch_shapes=scratch,
        compiler_params=pltpu.CompilerParams(
            dimension_semantics=("arbitrary", "arbitrary"),
            vmem_limit_bytes=VMEM_LIMIT),
        name="hybrid_layer_final" if final else "hybrid_layer",
    )(x, *params)


def kernel(x, norm_g, w_in, b_if, conv_qk_w, conv_qk_b, mhn_g, dw_w, dw_b, ln_g, ln_b,
           w_pa, w_pb, w_out, final_g):
    depth = norm_g.shape[0]
    row = lambda v: v.reshape(1, -1).astype(F32)
    for l in range(depth):
        w = w_in[l]
        wmain = jnp.concatenate([w[:, :_GATE_LO], w[:, _GATE_HI:]], axis=1).astype(BF16)
        wg = w[:, _GATE_LO:_GATE_HI].T
        z4 = jnp.zeros((4, D_MODEL), w.dtype)
        wif = jnp.concatenate([wg[0:4], z4, wg[4:8], z4], axis=0).astype(BF16)
        z41 = jnp.zeros((4, 1), F32)
        bi = jnp.concatenate([b_if[l, 0:4].reshape(4, 1), z41], axis=0)
        bf = jnp.concatenate([b_if[l, 4:8].reshape(4, 1), z41], axis=0)
        dww = jnp.concatenate([dw_w[l], jnp.zeros((1, D_B), dw_w.dtype)], axis=0)
        x = _layer_call(
            x, row(norm_g[l]), wmain, wif, bi, bf, conv_qk_w[l], row(conv_qk_b[l]), row(mhn_g[l]),
            dww, row(dw_b[l]), row(ln_g[l]), row(ln_b[l]),
            w_pa[l].astype(BF16), w_pb[l].astype(BF16), w_out[l].astype(BF16), row(final_g),
            final=(l == depth - 1))
    return x
```

```python
import functools

import jax
import jax.numpy as jnp
from jax import lax
from jax.experimental import pallas as pl
from jax.experimental.pallas import tpu as pltpu

D_MODEL = 1024
N_HEADS = 4
HEAD_DIM = 256
D_A = N_HEADS * HEAD_DIM
D_B = 1024
QK_CONV = 4
DW_K = 31
EPS = 1e-6

_GATE_LO = 4 * D_A
_GATE_HI = _GATE_LO + 2 * N_HEADS
G_Q, G_K, G_V, G_O, G_ZA, G_GLU_A, G_GLU_G, G_ZB, G_GA, G_GB = range(10)

TOK = 512
CHUNK = 256
QK_HIST = 8
DW_HIST = 32
XPOSE_ROWS = 128
ROWS = 32
NBLK = 256
LANES = 128
CONV_ROWS = 64
MROWS = 128
VMEM_LIMIT = 62 * 1024 * 1024

F32 = jnp.float32
BF16 = jnp.bfloat16


NEG_LOG2_E = -1.4426950408889634


def _sigmoid(v):
    return 1.0 / (1.0 + jnp.exp2(v * NEG_LOG2_E))


def _silu(v):
    return v * _sigmoid(v)


def _lane_scan(v, op, fill):
    n = v.shape[-1]
    lane = lax.broadcasted_iota(jnp.int32, v.shape, v.ndim - 1)
    sh = 1
    while sh < n:
        shifted = pltpu.roll(v, sh, v.ndim - 1)
        v = op(v, jnp.where(lane >= sh, shifted, fill))
        sh *= 2
    return v


def _rms(v, gain):
    return (v * lax.rsqrt(jnp.mean(v * v, axis=-1, keepdims=True) + EPS)) * gain


def _layer_body(x_ref, ng_ref, wmain_ref, wif_ref, bi_ref, bf_ref, cqw_ref, cqb_ref, mhn_ref,
                dww_ref, dwb_ref, lng_ref, lnb_ref, wpa_ref, wpb_ref, wout_ref, fg_ref,
                o_ref,
                hb, qkbuf, qc, kc, vb, ubuf, cvo, hhb, outa, ub, merged, cols,
                c_st, c_bf, n_st, m_st,
                *, final):
    T = TOK
    nb = D_A // NBLK
    spb = NBLK // LANES
    n_qslab = D_A // LANES
    n_chunks = T // CHUNK
    tile_rows = slice(QK_HIST, QK_HIST + T)
    conv_rows = slice(DW_HIST, DW_HIST + T)

    @pl.when(pl.program_id(1) == 0)
    def _reset():
        qkbuf[:, 0:QK_HIST, :] = jnp.zeros((qkbuf.shape[0], QK_HIST, LANES), F32)
        ubuf[:, 0:DW_HIST, :] = jnp.zeros((ubuf.shape[0], DW_HIST, LANES), F32)
        c_st[...] = jnp.zeros(c_st.shape, F32)
        c_bf[...] = jnp.zeros(c_bf.shape, BF16)
        n_st[...] = jnp.zeros(n_st.shape, F32)
        m_st[...] = jnp.zeros(m_st.shape, F32)

    def col_block(c):
        return slice(c * NBLK, (c + 1) * NBLK)

    def lane_block(i):
        return slice(i * LANES, (i + 1) * LANES)

    def proj(group, c):
        blk = group * nb + c
        return jnp.dot(hb[...], wmain_ref[:, blk * NBLK:(blk + 1) * NBLK],
                       preferred_element_type=F32)

    def to_slabs(buf, first, rows, val):
        for i in range(spb):
            buf[first + i, rows, :] = val[:, lane_block(i)]

    def scale_slabs(buf, first, rows, val):
        for i in range(spb):
            buf[first + i, rows, :] = buf[first + i, rows, :] * val[:, lane_block(i)]

    for i in range(T // ROWS):
        rs = slice(i * ROWS, (i + 1) * ROWS)
        hb[rs, :] = _rms(x_ref[rs, :], ng_ref[...]).astype(BF16)

    g_t = lax.dot_general(wif_ref[...], hb[...], (((1,), (1,)), ((), ())),
                          preferred_element_type=F32)
    gate_rows = []
    m_prev = m_st[:, 0:1]
    for ci in range(n_chunks):
        r0 = ci * CHUNK
        li = g_t[0:8, r0:r0 + CHUNK] + bi_ref[...]
        fp = g_t[8:16, r0:r0 + CHUNK] + bf_ref[...]
        lf = jnp.minimum(fp, 0.0) - jnp.log1p(jnp.exp(-jnp.abs(fp)))
        bcum = _lane_scan(lf, jnp.add, 0.0)
        a = li - bcum
        m_run = jnp.maximum(_lane_scan(a, jnp.maximum, -jnp.inf), m_prev)
        m_tok = bcum + m_run
        m_last = m_run[:, CHUNK - 1:CHUNK]
        stacked = jnp.concatenate(
            [m_run, jnp.exp(m_prev - m_run), jnp.exp(-m_tok), jnp.exp(a - m_last),
             jnp.zeros((XPOSE_ROWS - 32, CHUNK), F32)], axis=0)
        cols[ci] = stacked.T
        gate_rows.append((a, jnp.exp(m_prev - m_last)))
        m_prev = m_tok[:, CHUNK - 1:CHUNK]
    m_st[...] = jnp.broadcast_to(m_prev, m_st.shape)

    def qkconv_slab(s):
        dst, scale = (qc, HEAD_DIM ** -0.5) if s < n_qslab else (kc, None)
        ls = lane_block(s)
        ds_ = lane_block(s % n_qslab)
        for rb in range(T // CONV_ROWS):
            acc = jnp.broadcast_to(cqb_ref[:, ls], (CONV_ROWS, LANES))
            for j in range(QK_CONV):
                r0 = QK_HIST - (QK_CONV - 1) + j + rb * CONV_ROWS
                acc = acc + qkbuf[s, pl.ds(r0, CONV_ROWS), :] * cqw_ref[j:j + 1, ls]
            act = _silu(acc) if scale is None else _silu(acc) * scale
            dst[rb * CONV_ROWS:(rb + 1) * CONV_ROWS, ds_] = act.astype(BF16)

    def dwconv_unit(s, rb):
        ls = lane_block(s)
        half = CONV_ROWS // 2
        accs = [jnp.broadcast_to(dwb_ref[:, ls], (half, LANES))] * 2
        for j in range(DW_K):
            w = dww_ref[j:j + 1, ls]
            for parity in range(2):
                r0 = DW_HIST - (DW_K - 1) + j + rb * CONV_ROWS + parity
                accs[parity] = accs[parity] + ubuf[s, pl.ds(r0, half, stride=2), :] * w
        for parity in range(2):
            cvo[s, pl.ds(rb * CONV_ROWS + parity, half, stride=2), :] = accs[parity]

    dw_queue = [(s, rb) for s in range(D_B // LANES) for rb in range(T // CONV_ROWS)]

    def emit_dw(n):
        for _ in range(min(n, len(dw_queue))):
            dwconv_unit(*dw_queue.pop(0))

    def ln_rows(rb):
        rs = slice(rb * ROWS, (rb + 1) * ROWS)
        n_slab = D_B // LANES
        ys = [cvo[s, rs, :] for s in range(n_slab)]
        mu = jnp.sum(sum(ys), axis=-1, keepdims=True) * (1.0 / D_B)
        cens = [y - mu for y in ys]
        var = jnp.sum(sum(cn * cn for cn in cens), axis=-1, keepdims=True) * (1.0 / D_B)
        inv = lax.rsqrt(var + EPS)
        for s in range(n_slab):
            cvo[s, rs, :] = _silu(cens[s] * inv * lng_ref[:, lane_block(s)] + lnb_ref[:, lane_block(s)])

    def mlstm(ci, hd):
        r0 = ci * CHUNK
        a, a_carry = gate_rows[ci]
        hs = col_block(hd)
        k = kc[r0:r0 + CHUNK, hs]
        v = vb[r0:r0 + CHUNK, hs]
        col_iota = lax.broadcasted_iota(jnp.int32, (MROWS, CHUNK), 1)
        for part in range(CHUNK // MROWS):
            lr = slice(part * MROWS, (part + 1) * MROWS)
            rr = slice(r0 + part * MROWS, r0 + (part + 1) * MROWS)
            q = qc[rr, hs]
            m_col = cols[ci, lr, hd:hd + 1]
            ai_col = cols[ci, lr, 8 + hd:9 + hd]
            en_col = cols[ci, lr, 16 + hd:17 + hd]
            causal = (lax.broadcasted_iota(jnp.int32, (MROWS, CHUNK), 0) + part * MROWS) >= col_iota

            s = lax.dot_general(q, k, (((1,), (1,)), ((), ())), preferred_element_type=F32)
            p = s * jnp.where(causal, jnp.exp(a[hd:hd + 1, :] - m_col), 0.0)
            inter = jnp.dot(q, c_bf[hd], preferred_element_type=F32)
            num = jnp.dot(p.astype(BF16), v, preferred_element_type=F32) + ai_col * inter
            qn = jnp.sum(q.astype(F32) * n_st[hd:hd + 1, :], axis=1, keepdims=True)
            den = jnp.sum(p, axis=1, keepdims=True) + ai_col * qn
            hh = num / jnp.maximum(jnp.abs(den), en_col)
            hhb[rr, :] = _rms(hh, mhn_ref[:, hs])

        kw = k.astype(F32) * cols[ci, :, 24 + hd:25 + hd]
        upd = lax.dot_general(kw.astype(BF16), v, (((0,), (0,)), ((), ())),
                              preferred_element_type=F32)
        ac = a_carry[hd:hd + 1, :]
        c_new = ac * c_st[hd] + upd
        c_st[hd] = c_new
        c_bf[hd] = c_new.astype(BF16)
        n_st[hd:hd + 1, :] = ac * n_st[hd:hd + 1, :] + jnp.sum(kw, axis=0, keepdims=True)

    units_per_slab = T // CONV_ROWS
    for c in range(nb):
        to_slabs(ubuf, c * spb, conv_rows, proj(G_GLU_A, c))
        scale_slabs(ubuf, c * spb, conv_rows, _sigmoid(proj(G_GLU_G, c)))
        if c > 0:
            emit_dw(units_per_slab)
    per_dot = -(-len(dw_queue) // (2 * nb))
    for c in range(nb):
        to_slabs(qkbuf, c * spb, tile_rows, proj(G_Q, c))
        emit_dw(per_dot)
        to_slabs(qkbuf, (nb + c) * spb, tile_rows, proj(G_K, c))
        emit_dw(per_dot)
    ubuf[:, 0:DW_HIST, :] = ubuf[:, T:T + DW_HIST, :]
    for c in range(nb):
        vb[:, col_block(c)] = proj(G_V, c).astype(BF16)
        for i in range(spb):
            qkconv_slab(c * spb + i)
            qkconv_slab(n_qslab + c * spb + i)
    qkbuf[:, 0:QK_HIST, :] = qkbuf[:, T:T + QK_HIST, :]

    for hd in range(N_HEADS):
        for ci in range(n_chunks):
            mlstm(ci, hd)
        hhb[...] = hhb[...] * _sigmoid(proj(G_O, hd))
        outa[:, col_block(hd)] = (hhb[...] * _silu(proj(G_ZA, hd))).astype(BF16)
        to_slabs(qkbuf, hd * spb, tile_rows, _sigmoid(proj(G_GA, hd)))
        to_slabs(qkbuf, (nb + hd) * spb, tile_rows, _sigmoid(proj(G_GB, hd)))

    n_ln = T // ROWS
    for c in range(nb):
        scale_slabs(qkbuf, c * spb, tile_rows,
                    jnp.dot(outa[...], wpa_ref[:, col_block(c)], preferred_element_type=F32))
        for rb in range(c * n_ln // nb, (c + 1) * n_ln // nb):
            ln_rows(rb)
    for c in range(nb):
        gate = _silu(proj(G_ZB, c))
        for i in range(spb):
            ub[:, lane_block(c * spb + i)] = (cvo[c * spb + i] * gate[:, lane_block(i)]).astype(BF16)
    for c in range(nb):
        y_b = jnp.dot(ub[...], wpb_ref[:, col_block(c)], preferred_element_type=F32)
        for i in range(spb):
            merged[:, lane_block(c * spb + i)] = (
                qkbuf[c * spb + i, tile_rows, :]
                + qkbuf[(nb + c) * spb + i, tile_rows, :] * y_b[:, lane_block(i)]).astype(BF16)
    for c in range(D_MODEL // NBLK):
        cs = col_block(c)
        o_ref[:, cs] = x_ref[:, cs] + jnp.dot(merged[...], wout_ref[:, cs],
                                              preferred_element_type=F32)
    if final:
        for i in range(T // ROWS):
            rs = slice(i * ROWS, (i + 1) * ROWS)
            o_ref[rs, :] = _rms(o_ref[rs, :], fg_ref[...])


def _layer_call(x, ng, wmain, wif, bi, bf, cqw, cqb, mhn, dww, dwb, lng, lnb, wpa, wpb, wout, fg,
                *, final):
    B, S, D = x.shape
    T = TOK
    grid = (B, S // T)

    def whole(arr):
        nd = arr.ndim
        return pl.BlockSpec(arr.shape, lambda b, t: (0,) * nd, pipeline_mode=pl.Buffered(1))

    params = (ng, wmain, wif, bi, bf, cqw, cqb, mhn, dww, dwb, lng, lnb, wpa, wpb, wout, fg)
    tile = pl.BlockSpec((None, T, D), lambda b, t: (b, t, 0))
    scratch = [
        pltpu.VMEM((T, D_MODEL), BF16),
        pltpu.VMEM((2 * D_A // LANES, QK_HIST + T, LANES), F32),
        pltpu.VMEM((T, D_A), BF16),
        pltpu.VMEM((T, D_A), BF16),
        pltpu.VMEM((T, D_A), BF16),
        pltpu.VMEM((D_B // LANES, DW_HIST + T, LANES), F32),
        pltpu.VMEM((D_B // LANES, T, LANES), F32),
        pltpu.VMEM((T, HEAD_DIM), F32),
        pltpu.VMEM((T, D_A), BF16),
        pltpu.VMEM((T, D_B), BF16),
        pltpu.VMEM((T, D_MODEL), BF16),
        pltpu.VMEM((T // CHUNK, CHUNK, XPOSE_ROWS), F32),
        pltpu.VMEM((N_HEADS, HEAD_DIM, HEAD_DIM), F32),
        pltpu.VMEM((N_HEADS, HEAD_DIM, HEAD_DIM), BF16),
        pltpu.VMEM((8, HEAD_DIM), F32),
        pltpu.VMEM((8, 128), F32),
    ]
    return pl.pallas_call(
        functools.partial(_layer_body, final=final),
        grid=grid,
        in_specs=[tile] + [whole(p) for p in params],
        out_specs=tile,
        out_shape=jax.ShapeDtypeStruct(x.shape, x.dtype),
        scratch_shapes=scratch,
        compiler_params=pltpu.CompilerParams(
            dimension_semantics=("arbitrary", "arbitrary"),
            vmem_limit_bytes=VMEM_LIMIT),
        name="hybrid_layer_final" if final else "hybrid_layer",
    )(x, *params)


def kernel(x, norm_g, w_in, b_if, conv_qk_w, conv_qk_b, mhn_g, dw_w, dw_b, ln_g, ln_b,
           w_pa, w_pb, w_out, final_g):
    depth = norm_g.shape[0]
    row = lambda v: v.reshape(1, -1).astype(F32)
    for l in range(depth):
        w = w_in[l]
        wmain = jnp.concatenate([w[:, :_GATE_LO], w[:, _GATE_HI:]], axis=1).astype(BF16)
        wg = w[:, _GATE_LO:_GATE_HI].T
        z4 = jnp.zeros((4, D_MODEL), w.dtype)
        wif = jnp.concatenate([wg[0:4], z4, wg[4:8], z4], axis=0).astype(BF16)
        z41 = jnp.zeros((4, 1), F32)
        bi = jnp.concatenate([b_if[l, 0:4].reshape(4, 1), z41], axis=0)
        bf = jnp.concatenate([b_if[l, 4:8].reshape(4, 1), z41], axis=0)
        dww = jnp.concatenate([dw_w[l], jnp.zeros((1, D_B), dw_w.dtype)], axis=0)
        x = _layer_call(
            x, row(norm_g[l]), wmain, wif, bi, bf, conv_qk_w[l], row(conv_qk_b[l]), row(mhn_g[l]),
            dww, row(dw_b[l]), row(ln_g[l]), row(ln_b[l]),
            w_pa[l].astype(BF16), w_pb[l].astype(BF16), w_out[l].astype(BF16), row(final_g),
            final=(l == depth - 1))
    return x
```

```python
import functools

import jax
import jax.numpy as jnp
from jax import lax
from jax.experimental import pallas as pl
from jax.experimental.pallas import tpu as pltpu

D_MODEL = 1024
N_HEADS = 4
HEAD_DIM = 256
D_A = N_HEADS * HEAD_DIM
D_B = 1024
QK_CONV = 4
DW_K = 31
EPS = 1e-6

_GATE_LO = 4 * D_A
_GATE_HI = _GATE_LO + 2 * N_HEADS
G_Q, G_K, G_V, G_O, G_ZA, G_GLU_A, G_GLU_G, G_ZB, G_GA, G_GB = range(10)

TOK = 512
CHUNK = 256
QK_HIST = 8
DW_HIST = 32
XPOSE_ROWS = 128
ROWS = 32
NBLK = 256
LANES = 128
CONV_ROWS = 64
MROWS = 128
VMEM_LIMIT = 62 * 1024 * 1024

F32 = jnp.float32
BF16 = jnp.bfloat16


NEG_LOG2_E = -1.4426950408889634


def _sigmoid(v):
    return 1.0 / (1.0 + jnp.exp2(v * NEG_LOG2_E))


def _silu(v):
    return v * _sigmoid(v)


def _lane_scan(v, op, fill):
    n = v.shape[-1]
    lane = lax.broadcasted_iota(jnp.int32, v.shape, v.ndim - 1)
    sh = 1
    while sh < n:
        shifted = pltpu.roll(v, sh, v.ndim - 1)
        v = op(v, jnp.where(lane >= sh, shifted, fill))
        sh *= 2
    return v


def _rms(v, gain):
    return (v * lax.rsqrt(jnp.mean(v * v, axis=-1, keepdims=True) + EPS)) * gain


def _layer_body(x_ref, ng_ref, wmain_ref, wif_ref, bi_ref, bf_ref, cqw_ref, cqb_ref, mhn_ref,
                dww_ref, dwb_ref, lng_ref, lnb_ref, wpa_ref, wpb_ref, wout_ref, fg_ref,
                o_ref,
                hb, qkbuf, qc, kc, vb, ubuf, cvo, hhb, outa, ub, merged, cols,
                c_st, c_bf, n_st, m_st,
                *, final):
    T = TOK
    nb = D_A // NBLK
    spb = NBLK // LANES
    n_qslab = D_A // LANES
    n_chunks = T // CHUNK
    tile_rows = slice(QK_HIST, QK_HIST + T)
    conv_rows = slice(DW_HIST, DW_HIST + T)

    @pl.when(pl.program_id(1) == 0)
    def _reset():
        qkbuf[:, 0:QK_HIST, :] = jnp.zeros((qkbuf.shape[0], QK_HIST, LANES), F32)
        ubuf[:, 0:DW_HIST, :] = jnp.zeros((ubuf.shape[0], DW_HIST, LANES), F32)
        c_st[...] = jnp.zeros(c_st.shape, F32)
        c_bf[...] = jnp.zeros(c_bf.shape, BF16)
        n_st[...] = jnp.zeros(n_st.shape, F32)
        m_st[...] = jnp.zeros(m_st.shape, F32)

    def col_block(c):
        return slice(c * NBLK, (c + 1) * NBLK)

    def lane_block(i):
        return slice(i * LANES, (i + 1) * LANES)

    def proj(group, c):
        blk = group * nb + c
        return jnp.dot(hb[...], wmain_ref[:, blk * NBLK:(blk + 1) * NBLK],
                       preferred_element_type=F32)

    def to_slabs(buf, first, rows, val):
        for i in range(spb):
            buf[first + i, rows, :] = val[:, lane_block(i)]

    def scale_slabs(buf, first, rows, val):
        for i in range(spb):
            buf[first + i, rows, :] = buf[first + i, rows, :] * val[:, lane_block(i)]

    for i in range(T // ROWS):
        rs = slice(i * ROWS, (i + 1) * ROWS)
        hb[rs, :] = _rms(x_ref[rs, :], ng_ref[...]).astype(BF16)

    g_t = lax.dot_general(wif_ref[...], hb[...], (((1,), (1,)), ((), ())),
                          preferred_element_type=F32)
    gate_rows = []
    m_prev = m_st[:, 0:1]
    for ci in range(n_chunks):
        r0 = ci * CHUNK
        li = g_t[0:8, r0:r0 + CHUNK] + bi_ref[...]
        fp = g_t[8:16, r0:r0 + CHUNK] + bf_ref[...]
        lf = jnp.minimum(fp, 0.0) - jnp.log1p(jnp.exp(-jnp.abs(fp)))
        bcum = _lane_scan(lf, jnp.add, 0.0)
        a = li - bcum
        m_run = jnp.maximum(_lane_scan(a, jnp.maximum, -jnp.inf), m_prev)
        m_tok = bcum + m_run
        m_last = m_run[:, CHUNK - 1:CHUNK]
        stacked = jnp.concatenate(
            [m_run, jnp.exp(m_prev - m_run), jnp.exp(-m_tok), jnp.exp(a - m_last),
             jnp.zeros((XPOSE_ROWS - 32, CHUNK), F32)], axis=0)
        cols[ci] = stacked.T
        gate_rows.append((a, jnp.exp(m_prev - m_last)))
        m_prev = m_tok[:, CHUNK - 1:CHUNK]
    m_st[...] = jnp.broadcast_to(m_prev, m_st.shape)

    def qkconv_slab(s):
        dst, scale = (qc, HEAD_DIM ** -0.5) if s < n_qslab else (kc, None)
        ls = lane_block(s)
        ds_ = lane_block(s % n_qslab)
        for rb in range(T // CONV_ROWS):
            acc = jnp.broadcast_to(cqb_ref[:, ls], (CONV_ROWS, LANES))
            for j in range(QK_CONV):
                r0 = QK_HIST - (QK_CONV - 1) + j + rb * CONV_ROWS
                acc = acc + qkbuf[s, pl.ds(r0, CONV_ROWS), :] * cqw_ref[j:j + 1, ls]
            act = _silu(acc) if scale is None else _silu(acc) * scale
            dst[rb * CONV_ROWS:(rb + 1) * CONV_ROWS, ds_] = act.astype(BF16)

    def dwconv_unit(s, rb):
        ls = lane_block(s)
        half = CONV_ROWS // 2
        accs = [jnp.broadcast_to(dwb_ref[:, ls], (half, LANES))] * 2
        for j in range(DW_K):
            w = dww_ref[j:j + 1, ls]
            for parity in range(2):
                r0 = DW_HIST - (DW_K - 1) + j + rb * CONV_ROWS + parity
                accs[parity] = accs[parity] + ubuf[s, pl.ds(r0, half, stride=2), :] * w
        for parity in range(2):
            cvo[s, pl.ds(rb * CONV_ROWS + parity, half, stride=2), :] = accs[parity]

    dw_queue = [(s, rb) for s in range(D_B // LANES) for rb in range(T // CONV_ROWS)]

    def emit_dw(n):
        for _ in range(min(n, len(dw_queue))):
            dwconv_unit(*dw_queue.pop(0))

    def ln_rows(rb):
        rs = slice(rb * ROWS, (rb + 1) * ROWS)
        n_slab = D_B // LANES
        ys = [cvo[s, rs, :] for s in range(n_slab)]
        mu = jnp.sum(sum(ys), axis=-1, keepdims=True) * (1.0 / D_B)
        cens = [y - mu for y in ys]
        var = jnp.sum(sum(cn * cn for cn in cens), axis=-1, keepdims=True) * (1.0 / D_B)
        inv = lax.rsqrt(var + EPS)
        for s in range(n_slab):
            cvo[s, rs, :] = _silu(cens[s] * inv * lng_ref[:, lane_block(s)] + lnb_ref[:, lane_block(s)])

    def mlstm(ci, hd):
        r0 = ci * CHUNK
        a, a_carry = gate_rows[ci]
        hs = col_block(hd)
        k = kc[r0:r0 + CHUNK, hs]
        v = vb[r0:r0 + CHUNK, hs]
        col_iota = lax.broadcasted_iota(jnp.int32, (MROWS, CHUNK), 1)
        for part in range(CHUNK // MROWS):
            lr = slice(part * MROWS, (part + 1) * MROWS)
            rr = slice(r0 + part * MROWS, r0 + (part + 1) * MROWS)
            q = qc[rr, hs]
            m_col = cols[ci, lr, hd:hd + 1]
            ai_col = cols[ci, lr, 8 + hd:9 + hd]
            en_col = cols[ci, lr, 16 + hd:17 + hd]
            causal = (lax.broadcasted_iota(jnp.int32, (MROWS, CHUNK), 0) + part * MROWS) >= col_iota

            s = lax.dot_general(q, k, (((1,), (1,)), ((), ())), preferred_element_type=F32)
            p = s * jnp.where(causal, jnp.exp(a[hd:hd + 1, :] - m_col), 0.0)
            inter = jnp.dot(q, c_bf[hd], preferred_element_type=F32)
            num = jnp.dot(p.astype(BF16), v, preferred_element_type=F32) + ai_col * inter
            qn = jnp.sum(q.astype(F32) * n_st[hd:hd + 1, :], axis=1, keepdims=True)
            den = jnp.sum(p, axis=1, keepdims=True) + ai_col * qn
            hh = num / jnp.maximum(jnp.abs(den), en_col)
            hhb[rr, :] = _rms(hh, mhn_ref[:, hs])

        kw = k.astype(F32) * cols[ci, :, 24 + hd:25 + hd]
        upd = lax.dot_general(kw.astype(BF16), v, (((0,), (0,)), ((), ())),
                              preferred_element_type=F32)
        ac = a_carry[hd:hd + 1, :]
        c_new = ac * c_st[hd] + upd
        c_st[hd] = c_new
        c_bf[hd] = c_new.astype(BF16)
        n_st[hd:hd + 1, :] = ac * n_st[hd:hd + 1, :] + jnp.sum(kw, axis=0, keepdims=True)

    units_per_slab = T // CONV_ROWS
    for c in range(nb):
        to_slabs(ubuf, c * spb, conv_rows, proj(G_GLU_A, c))
        scale_slabs(ubuf, c * spb, conv_rows, _sigmoid(proj(G_GLU_G, c)))
        if c > 0:
            emit_dw(units_per_slab)
    per_dot = -(-len(dw_queue) // (2 * nb))
    for c in range(nb):
        to_slabs(qkbuf, c * spb, tile_rows, proj(G_Q, c))
        emit_dw(per_dot)
        to_slabs(qkbuf, (nb + c) * spb, tile_rows, proj(G_K, c))
        emit_dw(per_dot)
    ubuf[:, 0:DW_HIST, :] = ubuf[:, T:T + DW_HIST, :]
    for c in range(nb):
        vb[:, col_block(c)] = proj(G_V, c).astype(BF16)
        for i in range(spb):
            qkconv_slab(c * spb + i)
            qkconv_slab(n_qslab + c * spb + i)
    qkbuf[:, 0:QK_HIST, :] = qkbuf[:, T:T + QK_HIST, :]

    for hd in range(N_HEADS):
        for ci in range(n_chunks):
            mlstm(ci, hd)
        hhb[...] = hhb[...] * _sigmoid(proj(G_O, hd))
        outa[:, col_block(hd)] = (hhb[...] * _silu(proj(G_ZA, hd))).astype(BF16)
        to_slabs(qkbuf, hd * spb, tile_rows, _sigmoid(proj(G_GA, hd)))
        to_slabs(qkbuf, (nb + hd) * spb, tile_rows, _sigmoid(proj(G_GB, hd)))

    n_ln = T // ROWS
    for c in range(nb):
        scale_slabs(qkbuf, c * spb, tile_rows,
                    jnp.dot(outa[...], wpa_ref[:, col_block(c)], preferred_element_type=F32))
        for rb in range(c * n_ln // nb, (c + 1) * n_ln // nb):
            ln_rows(rb)
    for c in range(nb):
        gate = _silu(proj(G_ZB, c))
        for i in range(spb):
            ub[:, lane_block(c * spb + i)] = (cvo[c * spb + i] * gate[:, lane_block(i)]).astype(BF16)
    for c in range(nb):
        y_b = jnp.dot(ub[...], wpb_ref[:, col_block(c)], preferred_element_type=F32)
        for i in range(spb):
            merged[:, lane_block(c * spb + i)] = (
                qkbuf[c * spb + i, tile_rows, :]
                + qkbuf[(nb + c) * spb + i, tile_rows, :] * y_b[:, lane_block(i)]).astype(BF16)
    for c in range(D_MODEL // NBLK):
        cs = col_block(c)
        o_ref[:, cs] = x_ref[:, cs] + jnp.dot(merged[...], wout_ref[:, cs],
                                              preferred_element_type=F32)
    if final:
        for i in range(T // ROWS):
            rs = slice(i * ROWS, (i + 1) * ROWS)
            o_ref[rs, :] = _rms(o_ref[rs, :], fg_ref[...])


def _layer_call(x, ng, wmain, wif, bi, bf, cqw, cqb, mhn, dww, dwb, lng, lnb, wpa, wpb, wout, fg,
                *, layer, final):
    B, S, D = x.shape
    T = TOK
    grid = (B, S // T)

    def whole(arr):
        nd = arr.ndim
        return pl.BlockSpec(arr.shape, lambda b, t: (0,) * nd, pipeline_mode=pl.Buffered(1))

    def of_layer(arr):
        nd = arr.ndim
        return pl.BlockSpec((None,) + arr.shape[1:], lambda b, t: (layer,) + (0,) * (nd - 1),
                            pipeline_mode=pl.Buffered(1))

    params = (ng, wmain, wif, bi, bf, cqw, cqb, mhn, dww, dwb, lng, lnb, wpa, wpb, wout, fg)
    stacked = (wmain, wpa, wpb, wout)
    param_specs = [of_layer(p) if any(p is s for s in stacked) else whole(p) for p in params]
    tile = pl.BlockSpec((None, T, D), lambda b, t: (b, t, 0))
    scratch = [
        pltpu.VMEM((T, D_MODEL), BF16),
        pltpu.VMEM((2 * D_A // LANES, QK_HIST + T, LANES), F32),
        pltpu.VMEM((T, D_A), BF16),
        pltpu.VMEM((T, D_A), BF16),
        pltpu.VMEM((T, D_A), BF16),
        pltpu.VMEM((D_B // LANES, DW_HIST + T, LANES), F32),
        pltpu.VMEM((D_B // LANES, T, LANES), F32),
        pltpu.VMEM((T, HEAD_DIM), F32),
        pltpu.VMEM((T, D_A), BF16),
        pltpu.VMEM((T, D_B), BF16),
        pltpu.VMEM((T, D_MODEL), BF16),
        pltpu.VMEM((T // CHUNK, CHUNK, XPOSE_ROWS), F32),
        pltpu.VMEM((N_HEADS, HEAD_DIM, HEAD_DIM), F32),
        pltpu.VMEM((N_HEADS, HEAD_DIM, HEAD_DIM), BF16),
        pltpu.VMEM((8, HEAD_DIM), F32),
        pltpu.VMEM((8, 128), F32),
    ]
    return pl.pallas_call(
        functools.partial(_layer_body, final=final),
        grid=grid,
        in_specs=[tile] + param_specs,
        out_specs=tile,
        out_shape=jax.ShapeDtypeStruct(x.shape, x.dtype),
        scratch_shapes=scratch,
        compiler_params=pltpu.CompilerParams(
            dimension_semantics=("arbitrary", "arbitrary"),
            vmem_limit_bytes=VMEM_LIMIT),
        name="hybrid_layer_final" if final else "hybrid_layer",
    )(x, *params)


def kernel(x, norm_g, w_in, b_if, conv_qk_w, conv_qk_b, mhn_g, dw_w, dw_b, ln_g, ln_b,
           w_pa, w_pb, w_out, final_g):
    depth = norm_g.shape[0]
    row = lambda v: v.reshape(1, -1).astype(F32)
    wmain = jnp.concatenate([w_in[:, :, :_GATE_LO], w_in[:, :, _GATE_HI:]], axis=2).astype(BF16)
    wpa, wpb, wout = w_pa.astype(BF16), w_pb.astype(BF16), w_out.astype(BF16)
    for l in range(depth):
        w = w_in[l]
        wg = w[:, _GATE_LO:_GATE_HI].T
        z4 = jnp.zeros((4, D_MODEL), w.dtype)
        wif = jnp.concatenate([wg[0:4], z4, wg[4:8], z4], axis=0).astype(BF16)
        z41 = jnp.zeros((4, 1), F32)
        bi = jnp.concatenate([b_if[l, 0:4].reshape(4, 1), z41], axis=0)
        bf = jnp.concatenate([b_if[l, 4:8].reshape(4, 1), z41], axis=0)
        dww = jnp.concatenate([dw_w[l], jnp.zeros((1, D_B), dw_w.dtype)], axis=0)
        x = _layer_call(
            x, row(norm_g[l]), wmain, wif, bi, bf, conv_qk_w[l], row(conv_qk_b[l]), row(mhn_g[l]),
            dww, row(dw_b[l]), row(ln_g[l]), row(ln_b[l]),
            wpa, wpb, wout, row(final_g),
            layer=l, final=(l == depth - 1))
    return x
```
